```python
import math
import jax, jax.numpy as jnp
from jax import lax
import numpy as np

D_MODEL = 4096
BATCH = 4
SEQ = 2048
DEPTH = 1

N_META = 16
DA_HEADS = D_MODEL // 256
DA_QK_DIM = 64
DA_V_DIM = 2 * DA_QK_DIM
ROPE_DIMS = DA_QK_DIM // 4
ROPE_THETA = 500000.0
Q_BLOCK = 128
HG_HEADS = D_MODEL // 256
HG_K_DIM = 128
HG_V_DIM = 128
HG_CHUNK = 64
DA_QK_WIDTH = DA_HEADS * 2 * DA_QK_DIM
DA_V_WIDTH = DA_HEADS * DA_V_DIM
HG_K_WIDTH = HG_HEADS * HG_K_DIM
HG_V_WIDTH = HG_HEADS * HG_V_DIM
IN_SPLITS = (DA_QK_WIDTH, DA_QK_WIDTH, DA_V_WIDTH,
             HG_K_WIDTH, HG_K_WIDTH, HG_V_WIDTH, HG_V_WIDTH,
             D_MODEL, D_MODEL)
IN_WIDTH = sum(IN_SPLITS)
D_FF = 4 * D_MODEL
EPS = 1e-6
NEG_INF = -1e30

kernel_name = "hybrid_diffattn_hgrn2_gated_block"


def _rmsnorm(x, g):
    xf = x.astype(jnp.float32)
    r = lax.rsqrt(jnp.mean(xf * xf, axis=-1, keepdims=True) + EPS)
    return (xf * r).astype(x.dtype) * g.astype(x.dtype)


def _partial_rope(x, pos):
    half = ROPE_DIMS // 2
    inv_freq = ROPE_THETA ** (-(jnp.arange(half, dtype=jnp.float32) * 2.0) / ROPE_DIMS)
    ang = pos.astype(jnp.float32)[:, None] * inv_freq[None, :]
    cos = jnp.cos(ang)[:, None, None, :].astype(x.dtype)
    sin = jnp.sin(ang)[:, None, None, :].astype(x.dtype)
    x1 = x[..., :half]
    x2 = x[..., half:ROPE_DIMS]
    rest = x[..., ROPE_DIMS:]
    return jnp.concatenate([x1 * cos - x2 * sin, x2 * cos + x1 * sin, rest], axis=-1)


def _diff_attention(q, k, v, q_norm, k_norm, lam_q1, lam_k1, lam_q2, lam_k2, subln, layer_idx):
    B, T = q.shape[0], q.shape[1]
    pad = (-T) % Q_BLOCK
    L = pad + T
    nb = L // Q_BLOCK
    lam_init = 0.8 - 0.6 * math.exp(-0.3 * layer_idx)
    lam = (jnp.exp(jnp.sum(lam_q1.astype(jnp.float32) * lam_k1.astype(jnp.float32)))
           - jnp.exp(jnp.sum(lam_q2.astype(jnp.float32) * lam_k2.astype(jnp.float32)))
           + lam_init)
    q = _rmsnorm(q, q_norm)
    k = _rmsnorm(k, k_norm)
    q = jnp.pad(q, ((0, 0), (pad, 0), (0, 0), (0, 0), (0, 0)))
    k = jnp.pad(k, ((0, 0), (pad, 0), (0, 0), (0, 0), (0, 0)))
    v = jnp.pad(v, ((0, 0), (pad, 0), (0, 0), (0, 0)))
    pos = jnp.arange(L, dtype=jnp.int32) - pad
    q = _partial_rope(q, pos)
    k = _partial_rope(k, pos)
    scale = DA_QK_DIM ** -0.5
    key_pos = jnp.arange(L, dtype=jnp.int32)
    key_valid = key_pos >= pad
    q_blocks = q.reshape(B, nb, Q_BLOCK, DA_HEADS, 2, DA_QK_DIM).transpose(1, 0, 2, 3, 4, 5)

    def block(args):
        qb, bi = args
        s = jnp.einsum('bqhcd,bkhcd->bhcqk', qb, k,
                       preferred_element_type=jnp.float32) * scale
        qpos = bi * Q_BLOCK + jnp.arange(Q_BLOCK, dtype=jnp.int32)
        mask = (key_pos[None, :] <= qpos[:, None]) & key_valid[None, :]
        s = jnp.where(mask, s, NEG_INF)
        p = jax.nn.softmax(s, axis=-1)
        a = p[:, :, 0] - lam * p[:, :, 1]
        return jnp.einsum('bhqk,bkhe->bqhe', a.astype(v.dtype), v)

    out = lax.map(block, (q_blocks, jnp.arange(nb, dtype=jnp.int32)))
    out = out.transpose(1, 0, 2, 3, 4).reshape(B, L, DA_HEADS, DA_V_DIM)[:, pad:]
    out = _rmsnorm(out, subln) * (1.0 - lam_init)
    return out.reshape(B, T, DA_V_WIDTH)


def _hgrn2(q, f_logit, i, gate, lower_bound, out_norm, layer_idx):
    B, T = q.shape[0], q.shape[1]
    dt = q.dtype
    C = HG_CHUNK
    pad = (-T) % C
    L = pad + T
    nc = L // C
    lb_all = jnp.cumsum(jax.nn.softmax(lower_bound.astype(jnp.float32), axis=0), axis=0)
    lb = lb_all[layer_idx].reshape(HG_HEADS, HG_K_DIM)
    f = lb + (1.0 - lb) * jax.nn.sigmoid(f_logit.astype(jnp.float32))
    g = jnp.log(f)
    kk = 1.0 - f
    qf = q.astype(jnp.float32)
    vf = i.astype(jnp.float32)

    def to_chunks(a):
        a = jnp.pad(a, ((0, 0), (pad, 0), (0, 0), (0, 0)))
        return a.reshape(B, nc, C, HG_HEADS, a.shape[-1]).transpose(0, 3, 1, 2, 4)

    qc, gc, kc, vc = to_chunks(qf), to_chunks(g), to_chunks(kk), to_chunks(vf)
    b = jnp.cumsum(gc, axis=3)
    b_ref = b[:, :, :, C // 2 - 1:C // 2]
    b_last = b[:, :, :, C - 1:C]
    q_in = qc * jnp.exp(b - b_ref)
    k_in = kc * jnp.exp(b_ref - b)
    A = jnp.einsum('bhntd,bhnsd->bhnts', q_in, k_in)
    causal = jnp.tril(jnp.ones((C, C), dtype=bool))
    A = jnp.where(causal, A, 0.0)
    o_intra = jnp.einsum('bhnts,bhnsv->bhntv', A, vc)
    k_dec = kc * jnp.exp(b_last - b)
    U = jnp.einsum('bhnsd,bhnsv->bhndv', k_dec, vc)
    decay = jnp.exp(b_last[:, :, :, 0])

    def step(S, inp):
        U_c, d_c = inp
        return d_c[..., None] * S + U_c, S

    S0 = jnp.zeros((B, HG_HEADS, HG_K_DIM, HG_V_DIM), jnp.float32)
    _, S_start = lax.scan(step, S0, (U.transpose(2, 0, 1, 3, 4), decay.transpose(2, 0, 1, 3)))
    S_start = S_start.transpose(1, 2, 0, 3, 4)
    o_inter = jnp.einsum('bhntd,bhndv->bhntv', qc * jnp.exp(b), S_start)
    o = (o_intra + o_inter).transpose(0, 2, 3, 1, 4).reshape(B, L, HG_HEADS, HG_V_DIM)[:, pad:]
    o = _rmsnorm(o, out_norm.astype(jnp.float32)) * jax.nn.sigmoid(gate.astype(jnp.float32))
    return o.astype(dt).reshape(B, T, HG_V_WIDTH)


def setup_inputs(seed: int = 0) -> dict:
    key = jax.random.key(seed)
    ks = jax.random.split(key, 20)
    nrm = jax.random.normal
    f32 = jnp.float32
    return {
        "x": nrm(ks[0], (BATCH, SEQ, D_MODEL), f32),
        "meta_tokens": nrm(ks[1], (N_META, D_MODEL), f32),
        "norm_mix": 1.0 + 0.02 * nrm(ks[2], (DEPTH, D_MODEL), f32),
        "w_in": nrm(ks[3], (DEPTH, D_MODEL, IN_WIDTH), f32) * D_MODEL ** -0.5,
        "da_q_norm": 1.0 + 0.02 * nrm(ks[4], (DEPTH, DA_QK_DIM), f32),
        "da_k_norm": 1.0 + 0.02 * nrm(ks[5], (DEPTH, DA_QK_DIM), f32),
        "da_lambda_q1": 0.1 * nrm(ks[6], (DEPTH, DA_QK_DIM), f32),
        "da_lambda_k1": 0.1 * nrm(ks[7], (DEPTH, DA_QK_DIM), f32),
        "da_lambda_q2": 0.1 * nrm(ks[8], (DEPTH, DA_QK_DIM), f32),
        "da_lambda_k2": 0.1 * nrm(ks[9], (DEPTH, DA_QK_DIM), f32),
        "da_subln": 1.0 + 0.02 * nrm(ks[10], (DEPTH, DA_V_DIM), f32),
        "hg_lower_bound": 0.1 * nrm(ks[11], (DEPTH + 1, HG_K_WIDTH), f32),
        "hg_out_norm": 1.0 + 0.02 * nrm(ks[12], (DEPTH, HG_V_DIM), f32),
        "w_up_a": nrm(ks[13], (DEPTH, DA_V_WIDTH, D_MODEL), f32) * DA_V_WIDTH ** -0.5,
        "w_up_b": nrm(ks[14], (DEPTH, HG_V_WIDTH, D_MODEL), f32) * HG_V_WIDTH ** -0.5,
        "w_out": nrm(ks[15], (DEPTH, D_MODEL, D_MODEL), f32) * D_MODEL ** -0.5,
        "norm_mlp": 1.0 + 0.02 * nrm(ks[16], (DEPTH, D_MODEL), f32),
        "w_ff1": nrm(ks[17], (DEPTH, D_MODEL, D_FF), f32) * D_MODEL ** -0.5,
        "w_ff2": nrm(ks[18], (DEPTH, D_FF, D_MODEL), f32) * D_FF ** -0.5,
    }


def reference(x, meta_tokens, norm_mix, w_in, da_q_norm, da_k_norm, da_lambda_q1, da_lambda_k1,
              da_lambda_q2, da_lambda_k2, da_subln, hg_lower_bound, hg_out_norm, w_up_a, w_up_b,
              w_out, norm_mlp, w_ff1, w_ff2):
    B = x.shape[0]
    meta = jnp.broadcast_to(meta_tokens.astype(x.dtype)[None], (B, N_META, D_MODEL))
    h = jnp.concatenate([meta, x], axis=1)
    T = h.shape[1]
    split_points = tuple(int(s) for s in np.cumsum(IN_SPLITS)[:-1])
    for l in range(DEPTH):
        u = _rmsnorm(h, norm_mix[l])
        proj = jnp.einsum('btd,de->bte', u, w_in[l])
        da_q, da_k, da_v, hg_q, hg_f, hg_i, hg_g, gate_a, gate_b = jnp.split(proj, split_points, axis=-1)
        y_a = _diff_attention(
            da_q.reshape(B, T, DA_HEADS, 2, DA_QK_DIM),
            da_k.reshape(B, T, DA_HEADS, 2, DA_QK_DIM),
            da_v.reshape(B, T, DA_HEADS, DA_V_DIM),
            da_q_norm[l], da_k_norm[l], da_lambda_q1[l], da_lambda_k1[l],
            da_lambda_q2[l], da_lambda_k2[l], da_subln[l], l)
        y_b = _hgrn2(
            hg_q.reshape(B, T, HG_HEADS, HG_K_DIM),
            hg_f.reshape(B, T, HG_HEADS, HG_K_DIM),
            hg_i.reshape(B, T, HG_HEADS, HG_V_DIM),
            hg_g.reshape(B, T, HG_HEADS, HG_V_DIM),
            hg_lower_bound, hg_out_norm[l], l)
        y_a = jnp.einsum('bte,ed->btd', y_a, w_up_a[l])
        y_b = jnp.einsum('bte,ed->btd', y_b, w_up_b[l])
        merged = jax.nn.sigmoid(gate_a) * y_a + jax.nn.sigmoid(gate_b) * y_b
        h = h + jnp.einsum('btd,de->bte', merged, w_out[l])
        v = _rmsnorm(h, norm_mlp[l])
        hid = jnp.square(jax.nn.relu(jnp.einsum('btd,df->btf', v, w_ff1[l])))
        h = h + jnp.einsum('btf,fd->btd', hid, w_ff2[l])
    return h[:, N_META:]
```

```python
import functools
import math

import jax
import jax.numpy as jnp
from jax import lax
from jax.experimental import pallas as pl
from jax.experimental.pallas import tpu as pltpu

F32 = jnp.float32
BF16 = jnp.bfloat16
HIGHEST = lax.Precision.HIGHEST

D_MODEL = 4096
N_META = 16
HEADS = 16
HEAD_W = 128
QK_DIM = 64
ROPE_DIMS = 16
ROPE_THETA = 500000.0
HG_CHUNK = 64
META_ROWS = 128
EPS = 1e-6
NEG_INF = -1e30
LAM_INIT = 0.8 - 0.6 * math.exp(-0.3 * 0)

COL_DA_Q, COL_DA_K, COL_DA_V = 0, 16, 32
COL_HG_Q, COL_HG_F, COL_HG_I, COL_HG_G = 48, 64, 80, 96
COL_GATE_A, COL_GATE_B = 112 * HEAD_W, 144 * HEAD_W
IN_WIDTH = 176 * HEAD_W

V7X_VMEM_LIMIT_CAP = 58 * 2**20


def _params(semantics, block_bytes, extra_bytes=0):
    need = 2 * block_bytes + extra_bytes + (4 << 20)
    return pltpu.CompilerParams(dimension_semantics=semantics,
                                vmem_limit_bytes=min(max(need, 16 << 20), V7X_VMEM_LIMIT_CAP))


def _rmsnorm_kernel(x_ref, g_ref, o_ref):
    x = x_ref[...]
    r = lax.rsqrt(jnp.mean(x * x, axis=-1, keepdims=True) + EPS)
    o_ref[...] = ((x * r) * g_ref[...]).astype(o_ref.dtype)


def _rmsnorm(x, g, tm):
    m, d = x.shape
    return pl.pallas_call(
        _rmsnorm_kernel,
        grid=(m // tm,),
        in_specs=[pl.BlockSpec((tm, d), lambda i: (i, 0)),
                  pl.BlockSpec((1, d), lambda i: (0, 0))],
        out_specs=pl.BlockSpec((tm, d), lambda i: (i, 0)),
        out_shape=jax.ShapeDtypeStruct((m, d), BF16),
        compiler_params=_params(("parallel",), tm * d * 6),
        name="rmsnorm",
    )(x, g.reshape(1, d))


def _mm_kernel(*refs, nk, relu2, has_res):
    a_ref, w_ref = refs[0], refs[1]
    res_ref = refs[2] if has_res else None
    o_ref = refs[2 + has_res]
    part = jnp.dot(a_ref[...], w_ref[...], preferred_element_type=F32)

    def finish(acc):
        if relu2:
            acc = jnp.square(jnp.maximum(acc, 0.0))
        if has_res:
            acc = acc + res_ref[...]
        o_ref[...] = acc.astype(o_ref.dtype)

    if nk == 1:
        finish(part)
        return
    acc_ref = refs[3 + has_res]
    k = pl.program_id(2)

    @pl.when(k == 0)
    def _():
        acc_ref[...] = part

    @pl.when(k > 0)
    def _():
        acc_ref[...] += part

    @pl.when(k == nk - 1)
    def _():
        finish(acc_ref[...])


def _matmul(a, w, *, tm, tn, tk, out_dtype, relu2=False, res=None, name):
    m, kdim = a.shape
    n = w.shape[1]
    tm = min(tm, m)
    nk = kdim // tk
    in_specs = [pl.BlockSpec((tm, tk), lambda i, j, k: (i, k)),
                pl.BlockSpec((tk, tn), lambda i, j, k: (k, j))]
    args = [a, w]
    out_bytes = jnp.dtype(out_dtype).itemsize
    block_bytes = tm * tk * 2 + tk * tn * 2 + tm * tn * out_bytes
    if res is not None:
        in_specs.append(pl.BlockSpec((tm, tn), lambda i, j, k: (i, j)))
        args.append(res)
        block_bytes += tm * tn * 4
    scratch = [pltpu.VMEM((tm, tn), F32)] if nk > 1 else []
    return pl.pallas_call(
        functools.partial(_mm_kernel, nk=nk, relu2=relu2, has_res=res is not None),
        grid=(m // tm, n // tn, nk),
        in_specs=in_specs,
        out_specs=pl.BlockSpec((tm, tn), lambda i, j, k: (i, j)),
        out_shape=jax.ShapeDtypeStruct((m, n), out_dtype),
        scratch_shapes=scratch,
        compiler_params=_params(("parallel", "parallel", "arbitrary"), block_bytes,
                                2 * tm * tn * 4),
        name=name,
    )(*args)


def _qk_prep_kernel(x_ref, cos_ref, sin_lo_ref, sin_hi_ref, g_ref, o_ref):
    j = pl.program_id(1)
    x = x_ref[...].astype(F32)
    gi = lax.broadcasted_iota(jnp.int32, (HEAD_W, HEAD_W), 0) // QK_DIM
    gj = lax.broadcasted_iota(jnp.int32, (HEAD_W, HEAD_W), 1) // QK_DIM
    group_ones = (gi == gj).astype(F32)
    ss = jnp.dot(x * x, group_ones, precision=HIGHEST, preferred_element_type=F32)
    r = lax.rsqrt(ss * (1.0 / QK_DIM) + EPS)
    g = jnp.where(j < HEADS, g_ref[0:1, :], g_ref[1:2, :])
    y = (x * r) * g
    y = (y * cos_ref[...] + pltpu.roll(y, HEAD_W - ROPE_DIMS // 2, 1) * sin_lo_ref[...]
         + pltpu.roll(y, ROPE_DIMS // 2, 1) * sin_hi_ref[...])
    o_ref[...] = y.astype(o_ref.dtype)


def _qk_prep(proj, tables, gains, tm, rows_per_seq):
    m = proj.shape[0]
    nblk = rows_per_seq // tm
    tab_spec = pl.BlockSpec((tm, HEAD_W), lambda i, j: (i % nblk, 0))
    return pl.pallas_call(
        _qk_prep_kernel,
        grid=(m // tm, 2 * HEADS),
        in_specs=[pl.BlockSpec((tm, HEAD_W), lambda i, j: (i, j)),
                  tab_spec, tab_spec, tab_spec,
                  pl.BlockSpec((2, HEAD_W), lambda i, j: (0, 0))],
        out_specs=pl.BlockSpec((tm, HEAD_W), lambda i, j: (i, j)),
        out_shape=jax.ShapeDtypeStruct((m, 2 * HEADS * HEAD_W), BF16),
        compiler_params=_params(("parallel", "parallel"), tm * HEAD_W * 16, tm * HEAD_W * 32),
        name="qk_prep",
    )(proj, *tables, gains)


def _rope_tables(pos):
    half = ROPE_DIMS // 2
    inv_freq = ROPE_THETA ** (-(jnp.arange(half, dtype=F32) * 2.0) / ROPE_DIMS)
    ang = pos.astype(F32)[:, None] * inv_freq[None, :]
    cos, sin = jnp.cos(ang), jnp.sin(ang)
    n = pos.shape[0]
    pad = QK_DIM - ROPE_DIMS
    ones, zeros, zh = jnp.ones((n, pad), F32), jnp.zeros((n, pad), F32), jnp.zeros((n, half), F32)
    cos_t = jnp.concatenate([cos, cos, ones], axis=1)
    sin_lo = jnp.concatenate([-sin, zh, zeros], axis=1)
    sin_hi = jnp.concatenate([zh, sin, zeros], axis=1)
    return tuple(jnp.tile(t, (1, 2)) for t in (cos_t, sin_lo, sin_hi))


def _nt_dot(a, b):
    return lax.dot_general(a, b, (((1,), (1,)), ((), ())), preferred_element_type=F32)


def _softmax_step(s, vb, m, l, acc):
    m_new = jnp.maximum(m, jnp.max(s, axis=-1, keepdims=True))
    alpha = jnp.exp(m - m_new)
    p = jnp.exp(s - m_new)
    l = alpha * l + jnp.sum(p, axis=-1, keepdims=True)
    acc = alpha * acc + jnp.dot(p.astype(BF16), vb, preferred_element_type=F32)
    return m_new, l, acc


def _attn_kernel(q_ref, k_ref, v_ref, km_ref, vm_ref, lam_ref, subln_ref, o_ref, *, tq):
    qi = pl.program_id(2)
    q = q_ref[...]
    lane = lax.broadcasted_iota(jnp.int32, q.shape, 1)
    zero = jnp.zeros_like(q)
    q1 = jnp.where(lane < QK_DIM, q, zero)
    q2 = jnp.where(lane >= QK_DIM, q, zero)

    def step(carry, kb, vb, mask):
        m1, l1, a1, m2, l2, a2 = carry
        s1, s2 = _nt_dot(q1, kb), _nt_dot(q2, kb)
        if mask is not None:
            s1 = jnp.where(mask, s1, NEG_INF)
            s2 = jnp.where(mask, s2, NEG_INF)
        m1, l1, a1 = _softmax_step(s1, vb, m1, l1, a1)
        m2, l2, a2 = _softmax_step(s2, vb, m2, l2, a2)
        return m1, l1, a1, m2, l2, a2

    col_m = lax.broadcasted_iota(jnp.int32, (tq, META_ROWS), 1)
    m0 = jnp.full((tq, 1), NEG_INF, F32)
    l0 = jnp.zeros((tq, 1), F32)
    a0 = jnp.zeros((tq, HEAD_W), F32)
    carry = step((m0, l0, a0, m0, l0, a0), km_ref[...], vm_ref[...],
                 col_m >= META_ROWS - N_META)

    def body(j, c):
        off = pl.multiple_of(j * tq, tq)
        return step(c, k_ref[pl.ds(off, tq), :], v_ref[pl.ds(off, tq), :], None)

    carry = lax.fori_loop(0, qi, body, carry)
    off = pl.multiple_of(qi * tq, tq)
    row = lax.broadcasted_iota(jnp.int32, (tq, tq), 0)
    col = lax.broadcasted_iota(jnp.int32, (tq, tq), 1)
    m1, l1, a1, m2, l2, a2 = step(carry, k_ref[pl.ds(off, tq), :], v_ref[pl.ds(off, tq), :],
                                  col <= row)

    lp = lam_ref[...]
    lam = (jnp.exp(jnp.sum(lp[0:1] * lp[1:2], axis=-1, keepdims=True))
           - jnp.exp(jnp.sum(lp[2:3] * lp[3:4], axis=-1, keepdims=True)) + LAM_INIT)
    o = a1 / l1 - lam * (a2 / l2)
    r = lax.rsqrt(jnp.mean(o * o, axis=-1, keepdims=True) + EPS)
    o_ref[...] = (((o * r) * subln_ref[...]) * (1.0 - LAM_INIT)).astype(o_ref.dtype)


def _attention(qk, proj, qk_meta, proj_meta, lam_params, subln, batch, seq, tq):
    nq = seq // tq
    kernel = functools.partial(_attn_kernel, tq=tq)
    blk = (tq * HEAD_W * 2 + 2 * seq * HEAD_W * 2 + 2 * META_ROWS * HEAD_W * 2 + tq * HEAD_W * 2)
    return pl.pallas_call(
        kernel,
        grid=(batch, HEADS, nq),
        in_specs=[pl.BlockSpec((tq, HEAD_W), lambda b, h, i: (b * nq + i, h)),
                  pl.BlockSpec((seq, HEAD_W), lambda b, h, i: (b, HEADS + h)),
                  pl.BlockSpec((seq, HEAD_W), lambda b, h, i: (b, COL_DA_V + h)),
                  pl.BlockSpec((META_ROWS, HEAD_W), lambda b, h, i: (0, HEADS + h)),
                  pl.BlockSpec((META_ROWS, HEAD_W), lambda b, h, i: (0, COL_DA_V + h)),
                  pl.BlockSpec((4, QK_DIM), lambda b, h, i: (0, 0)),
                  pl.BlockSpec((1, HEAD_W), lambda b, h, i: (0, 0))],
        out_specs=pl.BlockSpec((tq, HEAD_W), lambda b, h, i: (b * nq + i, h)),
        out_shape=jax.ShapeDtypeStruct((batch * seq, HEADS * HEAD_W), BF16),
        compiler_params=_params(("parallel", "parallel", "arbitrary"), blk, 8 * tq * tq * 4),
        name="diff_attention",
    )(qk, qk, proj, qk_meta, proj_meta, lam_params, subln)


def _lower_bound(lb_ref):
    x = lb_ref[...]
    e = jnp.exp(x - jnp.max(x, axis=0, keepdims=True))
    return e[0:1] / jnp.sum(e, axis=0, keepdims=True)


def _hg_chunk(qc, fl, vc, lb, st, valid=None):
    c = HG_CHUNK
    f = lb + (1.0 - lb) * jax.nn.sigmoid(fl)
    g = jnp.log(f)
    kk = 1.0 - f
    if valid is not None:
        g = jnp.where(valid, g, 0.0)
        kk = jnp.where(valid, kk, 0.0)
        vc = jnp.where(valid, vc, 0.0)
    row = lax.broadcasted_iota(jnp.int32, (c, c), 0)
    col = lax.broadcasted_iota(jnp.int32, (c, c), 1)
    causal = col <= row
    b = jnp.dot(causal.astype(F32), g, precision=HIGHEST, preferred_element_type=F32)
    b_mid = b[c // 2 - 1:c // 2]
    b_last = b[c - 1:c]
    q_in = (qc * jnp.exp(b - b_mid)).astype(BF16)
    k_in = (kk * jnp.exp(b_mid - b)).astype(BF16)
    a = jnp.where(causal, _nt_dot(q_in, k_in), 0.0)
    vb = vc.astype(BF16)
    o = jnp.dot(a.astype(BF16), vb, preferred_element_type=F32)
    o = o + _nt_dot((qc * jnp.exp(b)).astype(BF16), st.astype(BF16))
    k_dec = (kk * jnp.exp(b_last - b)).astype(BF16)
    u_t = lax.dot_general(vb, k_dec, (((0,), (0,)), ((), ())), preferred_element_type=F32)
    return o, st * jnp.exp(b_last) + u_t


def _hg_meta_kernel(f_ref, i_ref, lb_ref, s_ref):
    row = lax.broadcasted_iota(jnp.int32, (HG_CHUNK, HEAD_W), 0)
    zeros = jnp.zeros((HG_CHUNK, HEAD_W), F32)
    _, st = _hg_chunk(zeros, f_ref[...].astype(F32), i_ref[...].astype(F32), _lower_bound(lb_ref),
                      jnp.zeros((HEAD_W, HEAD_W), F32), valid=row >= HG_CHUNK - N_META)
    s_ref[0] = st


def _hg_meta_state(proj_meta, lower_bound):
    blk = META_ROWS // HG_CHUNK - 1
    return pl.pallas_call(
        _hg_meta_kernel,
        grid=(HEADS,),
        in_specs=[pl.BlockSpec((HG_CHUNK, HEAD_W), lambda h: (blk, COL_HG_F + h)),
                  pl.BlockSpec((HG_CHUNK, HEAD_W), lambda h: (blk, COL_HG_I + h)),
                  pl.BlockSpec((lower_bound.shape[0], HEAD_W), lambda h: (0, h))],
        out_specs=pl.BlockSpec((1, HEAD_W, HEAD_W), lambda h: (h, 0, 0)),
        out_shape=jax.ShapeDtypeStruct((HEADS, HEAD_W, HEAD_W), F32),
        compiler_params=_params(("parallel",), 1 << 20),
        name="hgrn2_meta_state",
    )(proj_meta, proj_meta, lower_bound)


def _hgrn_kernel(q_ref, f_ref, i_ref, g_ref, lb_ref, s0_ref, on_ref, o_ref, st_ref, *, nchunk):
    @pl.when(pl.program_id(2) == 0)
    def _():
        st_ref[...] = s0_ref[0]

    lb = _lower_bound(lb_ref)
    st = st_ref[...]
    for c in range(nchunk):
        sl = pl.ds(c * HG_CHUNK, HG_CHUNK)
        o, st = _hg_chunk(q_ref[sl, :].astype(F32), f_ref[sl, :].astype(F32),
                          i_ref[sl, :].astype(F32), lb, st)
        r = lax.rsqrt(jnp.mean(o * o, axis=-1, keepdims=True) + EPS)
        y = ((o * r) * on_ref[...]) * jax.nn.sigmoid(g_ref[sl, :].astype(F32))
        o_ref[sl, :] = y.astype(o_ref.dtype)
    st_ref[...] = st


def _hgrn2(proj, s0, lower_bound, out_norm, batch, seq, tb):
    nt = seq // tb
    kernel = functools.partial(_hgrn_kernel, nchunk=tb // HG_CHUNK)

    def col(off):
        return pl.BlockSpec((tb, HEAD_W), lambda b, h, t: (b * nt + t, off + h))

    return pl.pallas_call(
        kernel,
        grid=(batch, HEADS, nt),
        in_specs=[col(COL_HG_Q), col(COL_HG_F), col(COL_HG_I), col(COL_HG_G),
                  pl.BlockSpec((lower_bound.shape[0], HEAD_W), lambda b, h, t: (0, h)),
                  pl.BlockSpec((1, HEAD_W, HEAD_W), lambda b, h, t: (h, 0, 0)),
                  pl.BlockSpec((1, HEAD_W), lambda b, h, t: (0, 0))],
        out_specs=pl.BlockSpec((tb, HEAD_W), lambda b, h, t: (b * nt + t, h)),
        out_shape=jax.ShapeDtypeStruct((batch * seq, HEADS * HEAD_W), BF16),
        scratch_shapes=[pltpu.VMEM((HEAD_W, HEAD_W), F32)],
        compiler_params=_params(("parallel", "parallel", "arbitrary"), 5 * tb * HEAD_W * 2,
                                16 * tb * HEAD_W * 4),
        name="hgrn2",
    )(proj, proj, proj, proj, lower_bound, s0, out_norm)


def _merge_kernel(ya_ref, yb_ref, wa_ref, wb_ref, ga_ref, gb_ref, o_ref):
    a = jnp.dot(ya_ref[...], wa_ref[...], preferred_element_type=F32)
    b = jnp.dot(yb_ref[...], wb_ref[...], preferred_element_type=F32)
    o = (jax.nn.sigmoid(ga_ref[...].astype(F32)) * a + jax.nn.sigmoid(gb_ref[...].astype(F32)) * b)
    o_ref[...] = o.astype(o_ref.dtype)


def _gated_merge(y_a, y_b, w_a, w_b, proj, tm, tn):
    m, kdim = y_a.shape
    n = w_a.shape[1]
    tm = min(tm, m)
    blk = 2 * tm * kdim * 2 + 2 * kdim * tn * 2 + 3 * tm * tn * 2
    return pl.pallas_call(
        _merge_kernel,
        grid=(m // tm, n // tn),
        in_specs=[pl.BlockSpec((tm, kdim), lambda i, j: (i, 0)),
                  pl.BlockSpec((tm, kdim), lambda i, j: (i, 0)),
                  pl.BlockSpec((kdim, tn), lambda i, j: (0, j)),
                  pl.BlockSpec((kdim, tn), lambda i, j: (0, j)),
                  pl.BlockSpec((tm, tn), lambda i, j: (i, COL_GATE_A // tn + j)),
                  pl.BlockSpec((tm, tn), lambda i, j: (i, COL_GATE_B // tn + j))],
        out_specs=pl.BlockSpec((tm, tn), lambda i, j: (i, j)),
        out_shape=jax.ShapeDtypeStruct((m, n), BF16),
        compiler_params=_params(("parallel", "parallel"), blk, 4 * tm * tn * 4),
        name="gated_merge",
    )(y_a, y_b, w_a, w_b, proj, proj)


def kernel(x, meta_tokens, norm_mix, w_in, da_q_norm, da_k_norm, da_lambda_q1, da_lambda_k1,
           da_lambda_q2, da_lambda_k2, da_subln, hg_lower_bound, hg_out_norm, w_up_a, w_up_b,
           w_out, norm_mlp, w_ff1, w_ff2):
    batch, seq, d = x.shape
    assert d == D_MODEL and w_in.shape[0] == 1 and w_in.shape[2] == IN_WIDTH
    assert meta_tokens.shape == (N_META, D_MODEL) and seq % 512 == 0
    m = batch * seq
    xf = x.reshape(m, d)
    meta_pad = jnp.concatenate([jnp.zeros((META_ROWS - N_META, d), F32), meta_tokens.astype(F32)])

    w_in_b = w_in[0].astype(BF16)
    w_a_b, w_b_b = w_up_a[0].astype(BF16), w_up_b[0].astype(BF16)
    w_out_b, w_ff1_b, w_ff2_b = w_out[0].astype(BF16), w_ff1[0].astype(BF16), w_ff2[0].astype(BF16)

    u = _rmsnorm(xf, norm_mix[0], 256)
    u_meta = _rmsnorm(meta_pad, norm_mix[0], META_ROWS)
    proj = _matmul(u, w_in_b, tm=1024, tn=1024, tk=d, out_dtype=BF16, name="in_proj")
    proj_meta = _matmul(u_meta, w_in_b, tm=META_ROWS, tn=1024, tk=d, out_dtype=BF16,
                        name="in_proj_meta")

    gains = jnp.stack([jnp.tile(da_q_norm[0], 2) * QK_DIM ** -0.5, jnp.tile(da_k_norm[0], 2)])
    pos_meta = jnp.maximum(jnp.arange(META_ROWS, dtype=jnp.int32) - (META_ROWS - N_META), 0)
    pos_main = jnp.arange(seq, dtype=jnp.int32) + N_META
    qk = _qk_prep(proj, _rope_tables(pos_main), gains, 512, seq)
    qk_meta = _qk_prep(proj_meta, _rope_tables(pos_meta), gains, META_ROWS, META_ROWS)
    lam_params = jnp.stack([da_lambda_q1[0], da_lambda_k1[0], da_lambda_q2[0], da_lambda_k2[0]])
    y_a = _attention(qk, proj, qk_meta, proj_meta, lam_params, da_subln[0].reshape(1, HEAD_W),
                     batch, seq, 512)

    s0 = _hg_meta_state(proj_meta, hg_lower_bound)
    y_b = _hgrn2(proj, s0, hg_lower_bound, hg_out_norm[0].reshape(1, HEAD_W), batch, seq, 512)

    merged = _gated_merge(y_a, y_b, w_a_b, w_b_b, proj, 1024, 512)
    h1 = _matmul(merged, w_out_b, tm=1024, tn=1024, tk=d, out_dtype=F32, res=xf, name="out_proj")

    v = _rmsnorm(h1, norm_mlp[0], 256)
    hid = _matmul(v, w_ff1_b, tm=1024, tn=1024, tk=d, out_dtype=BF16, relu2=True, name="ff1")
    out = _matmul(hid, w_ff2_b, tm=1024, tn=1024, tk=2048, out_dtype=F32, res=h1, name="ff2")
    return out.reshape(batch, seq, d)
```

```python
import functools
import math

import jax
import jax.numpy as jnp
from jax import lax
from jax.experimental import pallas as pl
from jax.experimental.pallas import tpu as pltpu

F32 = jnp.float32
BF16 = jnp.bfloat16

D_MODEL = 4096
N_META = 16
HEADS = 16
HEAD_W = 128
QK_DIM = 64
ROPE_DIMS = 16
ROPE_THETA = 500000.0
HG_CHUNK = 64
META_ROWS = 128
EPS = 1e-6
NEG_INF = -1e30
LAM_INIT = 0.8 - 0.6 * math.exp(-0.3 * 0)

COL_DA_Q, COL_DA_K, COL_DA_V = 0, 16, 32
COL_HG_Q, COL_HG_F, COL_HG_I, COL_HG_G = 48, 64, 80, 96
COL_GATE_A, COL_GATE_B = 112 * HEAD_W, 144 * HEAD_W
IN_WIDTH = 176 * HEAD_W

V7X_VMEM_LIMIT_CAP = 58 * 2**20


def _params(semantics, block_bytes, extra_bytes=0):
    need = 2 * block_bytes + extra_bytes + (4 << 20)
    return pltpu.CompilerParams(dimension_semantics=semantics,
                                vmem_limit_bytes=min(max(need, 16 << 20), V7X_VMEM_LIMIT_CAP))


def _rmsnorm_kernel(x_ref, g_ref, o_ref):
    x = x_ref[...]
    r = lax.rsqrt(jnp.mean(x * x, axis=-1, keepdims=True) + EPS)
    o_ref[...] = ((x * r) * g_ref[...]).astype(o_ref.dtype)


def _rmsnorm(x, g, tm):
    m, d = x.shape
    return pl.pallas_call(
        _rmsnorm_kernel,
        grid=(m // tm,),
        in_specs=[pl.BlockSpec((tm, d), lambda i: (i, 0)),
                  pl.BlockSpec((1, d), lambda i: (0, 0))],
        out_specs=pl.BlockSpec((tm, d), lambda i: (i, 0)),
        out_shape=jax.ShapeDtypeStruct((m, d), BF16),
        compiler_params=_params(("parallel",), tm * d * 6),
        name="rmsnorm",
    )(x, g.reshape(1, d))


def _mm_kernel(*refs, nk, relu2, has_res):
    a_ref, w_ref = refs[0], refs[1]
    res_ref = refs[2] if has_res else None
    o_ref = refs[2 + has_res]
    part = jnp.dot(a_ref[...], w_ref[...], preferred_element_type=F32)

    def finish(acc):
        if relu2:
            acc = jnp.square(jnp.maximum(acc, 0.0))
        if has_res:
            acc = acc + res_ref[...]
        o_ref[...] = acc.astype(o_ref.dtype)

    if nk == 1:
        finish(part)
        return
    acc_ref = refs[3 + has_res]
    k = pl.program_id(2)

    @pl.when(k == 0)
    def _():
        acc_ref[...] = part

    @pl.when(k > 0)
    def _():
        acc_ref[...] += part

    @pl.when(k == nk - 1)
    def _():
        finish(acc_ref[...])


def _matmul(a, w, *, tm, tn, tk, out_dtype, relu2=False, res=None, name):
    m, kdim = a.shape
    n = w.shape[1]
    tm = min(tm, m)
    nk = kdim // tk
    in_specs = [pl.BlockSpec((tm, tk), lambda i, j, k: (i, k)),
                pl.BlockSpec((tk, tn), lambda i, j, k: (k, j))]
    args = [a, w]
    out_bytes = jnp.dtype(out_dtype).itemsize
    block_bytes = tm * tk * 2 + tk * tn * 2 + tm * tn * out_bytes
    if res is not None:
        in_specs.append(pl.BlockSpec((tm, tn), lambda i, j, k: (i, j)))
        args.append(res)
        block_bytes += tm * tn * 4
    scratch = [pltpu.VMEM((tm, tn), F32)] if nk > 1 else []
    return pl.pallas_call(
        functools.partial(_mm_kernel, nk=nk, relu2=relu2, has_res=res is not None),
        grid=(m // tm, n // tn, nk),
        in_specs=in_specs,
        out_specs=pl.BlockSpec((tm, tn), lambda i, j, k: (i, j)),
        out_shape=jax.ShapeDtypeStruct((m, n), out_dtype),
        scratch_shapes=scratch,
        compiler_params=_params(("parallel", "parallel", "arbitrary"), block_bytes,
                                2 * tm * tn * 4),
        name=name,
    )(*args)


def _split_terms(x, n):
    terms = []
    for _ in range(n - 1):
        t = x.astype(BF16)
        terms.append(t)
        x = x - t.astype(F32)
    terms.append(x.astype(BF16))
    return terms


def _qk_prep_kernel(x_ref, cos_ref, sin_lo_ref, sin_hi_ref, g_ref, ones_ref, o_ref, *,
                    heads_per_block):
    j = pl.program_id(1)
    g = jnp.where(j * heads_per_block < HEADS, g_ref[0:1, :], g_ref[1:2, :])
    cos, sin_lo, sin_hi = cos_ref[...], sin_lo_ref[...], sin_hi_ref[...]
    for hh in range(heads_per_block):
        sl = slice(hh * HEAD_W, (hh + 1) * HEAD_W)
        x = x_ref[:, sl].astype(F32)
        sq = jnp.concatenate(_split_terms(x * x, 2), axis=1)
        ss = jnp.dot(sq, ones_ref[...], preferred_element_type=F32)
        r = lax.rsqrt(ss * (1.0 / QK_DIM) + EPS)
        y = (x * r) * g
        y = (y * cos + pltpu.roll(y, HEAD_W - ROPE_DIMS // 2, 1) * sin_lo
             + pltpu.roll(y, ROPE_DIMS // 2, 1) * sin_hi)
        o_ref[:, sl] = y.astype(o_ref.dtype)


def _qk_prep(proj, tables, gains, tm, rows_per_seq, heads_per_block):
    m = proj.shape[0]
    nblk = rows_per_seq // tm
    tw = heads_per_block * HEAD_W
    lane_group = jnp.arange(HEAD_W) // QK_DIM
    group_ones = jnp.tile((lane_group[:, None] == lane_group[None, :]).astype(BF16), (2, 1))
    tab_spec = pl.BlockSpec((tm, HEAD_W), lambda i, j: (i % nblk, 0))
    return pl.pallas_call(
        functools.partial(_qk_prep_kernel, heads_per_block=heads_per_block),
        grid=(m // tm, 2 * HEADS // heads_per_block),
        in_specs=[pl.BlockSpec((tm, tw), lambda i, j: (i, j)),
                  tab_spec, tab_spec, tab_spec,
                  pl.BlockSpec((2, HEAD_W), lambda i, j: (0, 0)),
                  pl.BlockSpec((2 * HEAD_W, HEAD_W), lambda i, j: (0, 0))],
        out_specs=pl.BlockSpec((tm, tw), lambda i, j: (i, j)),
        out_shape=jax.ShapeDtypeStruct((m, 2 * HEADS * HEAD_W), BF16),
        compiler_params=_params(("parallel", "parallel"), tm * tw * 4 + 3 * tm * HEAD_W * 4,
                                tm * HEAD_W * 48),
        name="qk_prep",
    )(proj, *tables, gains, group_ones)


def _rope_tables(pos):
    half = ROPE_DIMS // 2
    inv_freq = ROPE_THETA ** (-(jnp.arange(half, dtype=F32) * 2.0) / ROPE_DIMS)
    ang = pos.astype(F32)[:, None] * inv_freq[None, :]
    cos, sin = jnp.cos(ang), jnp.sin(ang)
    n = pos.shape[0]
    pad = QK_DIM - ROPE_DIMS
    ones, zeros, zh = jnp.ones((n, pad), F32), jnp.zeros((n, pad), F32), jnp.zeros((n, half), F32)
    cos_t = jnp.concatenate([cos, cos, ones], axis=1)
    sin_lo = jnp.concatenate([-sin, zh, zeros], axis=1)
    sin_hi = jnp.concatenate([zh, sin, zeros], axis=1)
    return tuple(jnp.tile(t, (1, 2)) for t in (cos_t, sin_lo, sin_hi))


def _nt_dot(a, b):
    return lax.dot_general(a, b, (((1,), (1,)), ((), ())), preferred_element_type=F32)


def _softmax_step(s, vb, m, l, acc):
    m_new = jnp.maximum(m, jnp.max(s, axis=-1, keepdims=True))
    alpha = jnp.exp(m - m_new)
    p = jnp.exp(s - m_new)
    l = alpha * l + jnp.sum(p, axis=-1, keepdims=True)
    acc = alpha * acc + jnp.dot(p.astype(BF16), vb, preferred_element_type=F32)
    return m_new, l, acc


def _attn_kernel(q_ref, k_ref, v_ref, km_ref, vm_ref, lam_ref, subln_ref, o_ref, *, tq):
    qi = pl.program_id(2)
    q = q_ref[...]
    lane = lax.broadcasted_iota(jnp.int32, q.shape, 1)
    zero = jnp.zeros_like(q)
    q1 = jnp.where(lane < QK_DIM, q, zero)
    q2 = jnp.where(lane >= QK_DIM, q, zero)

    def step(carry, kb, vb, mask):
        m1, l1, a1, m2, l2, a2 = carry
        s1, s2 = _nt_dot(q1, kb), _nt_dot(q2, kb)
        if mask is not None:
            s1 = jnp.where(mask, s1, NEG_INF)
            s2 = jnp.where(mask, s2, NEG_INF)
        m1, l1, a1 = _softmax_step(s1, vb, m1, l1, a1)
        m2, l2, a2 = _softmax_step(s2, vb, m2, l2, a2)
        return m1, l1, a1, m2, l2, a2

    col_m = lax.broadcasted_iota(jnp.int32, (tq, META_ROWS), 1)
    m0 = jnp.full((tq, 1), NEG_INF, F32)
    l0 = jnp.zeros((tq, 1), F32)
    a0 = jnp.zeros((tq, HEAD_W), F32)
    carry = step((m0, l0, a0, m0, l0, a0), km_ref[...], vm_ref[...],
                 col_m >= META_ROWS - N_META)

    def body(j, c):
        off = pl.multiple_of(j * tq, tq)
        return step(c, k_ref[pl.ds(off, tq), :], v_ref[pl.ds(off, tq), :], None)

    carry = lax.fori_loop(0, qi, body, carry)
    off = pl.multiple_of(qi * tq, tq)
    row = lax.broadcasted_iota(jnp.int32, (tq, tq), 0)
    col = lax.broadcasted_iota(jnp.int32, (tq, tq), 1)
    m1, l1, a1, m2, l2, a2 = step(carry, k_ref[pl.ds(off, tq), :], v_ref[pl.ds(off, tq), :],
                                  col <= row)

    lp = lam_ref[...]
    lam = (jnp.exp(jnp.sum(lp[0:1] * lp[1:2], axis=-1, keepdims=True))
           - jnp.exp(jnp.sum(lp[2:3] * lp[3:4], axis=-1, keepdims=True)) + LAM_INIT)
    o = a1 / l1 - lam * (a2 / l2)
    r = lax.rsqrt(jnp.mean(o * o, axis=-1, keepdims=True) + EPS)
    o_ref[...] = (((o * r) * subln_ref[...]) * (1.0 - LAM_INIT)).astype(o_ref.dtype)


def _attention(qk, proj, qk_meta, proj_meta, lam_params, subln, batch, seq, tq):
    nq = seq // tq
    kernel = functools.partial(_attn_kernel, tq=tq)
    blk = (tq * HEAD_W * 2 + 2 * seq * HEAD_W * 2 + 2 * META_ROWS * HEAD_W * 2 + tq * HEAD_W * 2)
    return pl.pallas_call(
        kernel,
        grid=(batch, HEADS, nq),
        in_specs=[pl.BlockSpec((tq, HEAD_W), lambda b, h, i: (b * nq + i, h)),
                  pl.BlockSpec((seq, HEAD_W), lambda b, h, i: (b, HEADS + h)),
                  pl.BlockSpec((seq, HEAD_W), lambda b, h, i: (b, COL_DA_V + h)),
                  pl.BlockSpec((META_ROWS, HEAD_W), lambda b, h, i: (0, HEADS + h)),
                  pl.BlockSpec((META_ROWS, HEAD_W), lambda b, h, i: (0, COL_DA_V + h)),
                  pl.BlockSpec((4, QK_DIM), lambda b, h, i: (0, 0)),
                  pl.BlockSpec((1, HEAD_W), lambda b, h, i: (0, 0))],
        out_specs=pl.BlockSpec((tq, HEAD_W), lambda b, h, i: (b * nq + i, h)),
        out_shape=jax.ShapeDtypeStruct((batch * seq, HEADS * HEAD_W), BF16),
        compiler_params=_params(("parallel", "parallel", "arbitrary"), blk, 8 * tq * tq * 4),
        name="diff_attention",
    )(qk, qk, proj, qk_meta, proj_meta, lam_params, subln)


def _lower_bound(lb_ref):
    x = lb_ref[...]
    e = jnp.exp(x - jnp.max(x, axis=0, keepdims=True))
    return e[0:1] / jnp.sum(e, axis=0, keepdims=True)


def _chunk_tri(rows):
    r = jnp.arange(rows)
    same_chunk = r[:, None] // HG_CHUNK == r[None, :] // HG_CHUNK
    return (same_chunk & (r[None, :] <= r[:, None])).astype(BF16)


def _chunk_cumsum(tri, g):
    parts = jnp.concatenate(_split_terms(g, 3), axis=1)
    s = jnp.dot(tri, parts, preferred_element_type=F32)
    return s[:, :HEAD_W] + s[:, HEAD_W:2 * HEAD_W] + s[:, 2 * HEAD_W:]


def _hg_block(q, fl, v, lb, tri, st, valid=None):
    c = HG_CHUNK
    n = q.shape[0] // c
    f = lb + (1.0 - lb) * jax.nn.sigmoid(fl)
    g = jnp.log(f)
    kk = 1.0 - f
    if valid is not None:
        g = jnp.where(valid, g, 0.0)
        kk = jnp.where(valid, kk, 0.0)
        v = jnp.where(valid, v, jnp.zeros_like(v))
    b = _chunk_cumsum(tri, g).reshape(n, c, HEAD_W)
    q3, k3, v3 = q.reshape(n, c, HEAD_W), kk.reshape(n, c, HEAD_W), v.reshape(n, c, HEAD_W)
    b_mid = b[:, c // 2 - 1:c // 2, :]
    b_last = b[:, c - 1:c, :]
    q_in = (q3 * jnp.exp(b - b_mid)).astype(BF16)
    k_in = (k3 * jnp.exp(b_mid - b)).astype(BF16)
    k_dec = (k3 * jnp.exp(b_last - b)).astype(BF16)
    q_ex = (q3 * jnp.exp(b)).astype(BF16)
    decay = jnp.exp(b_last)
    row = lax.broadcasted_iota(jnp.int32, (n, c, c), 1)
    col = lax.broadcasted_iota(jnp.int32, (n, c, c), 2)
    a = lax.dot_general(q_in, k_in, (((2,), (2,)), ((0,), (0,))), preferred_element_type=F32)
    a = jnp.where(col <= row, a, 0.0).astype(BF16)
    o_intra = lax.dot_general(a, v3, (((2,), (1,)), ((0,), (0,))), preferred_element_type=F32)
    u_t = [lax.dot_general(v3[i], k_dec[i], (((0,), (0,)), ((), ())), preferred_element_type=F32)
           for i in range(n)]
    outs = []
    for i in range(n):
        outs.append(o_intra[i] + _nt_dot(q_ex[i], st.astype(BF16)))
        st = st * decay[i] + u_t[i]
    return outs, st


def _hg_meta_kernel(f_ref, i_ref, tri_ref, lb_ref, s_ref):
    row = lax.broadcasted_iota(jnp.int32, (HG_CHUNK, HEAD_W), 0)
    _, st = _hg_block(jnp.zeros((HG_CHUNK, HEAD_W), F32), f_ref[...].astype(F32), i_ref[...],
                      _lower_bound(lb_ref), tri_ref[...], jnp.zeros((HEAD_W, HEAD_W), F32),
                      valid=row >= HG_CHUNK - N_META)
    s_ref[0] = st


def _hg_meta_state(proj_meta, lower_bound):
    blk = META_ROWS // HG_CHUNK - 1
    return pl.pallas_call(
        _hg_meta_kernel,
        grid=(HEADS,),
        in_specs=[pl.BlockSpec((HG_CHUNK, HEAD_W), lambda h: (blk, COL_HG_F + h)),
                  pl.BlockSpec((HG_CHUNK, HEAD_W), lambda h: (blk, COL_HG_I + h)),
                  pl.BlockSpec((HG_CHUNK, HG_CHUNK), lambda h: (0, 0)),
                  pl.BlockSpec((lower_bound.shape[0], HEAD_W), lambda h: (0, h))],
        out_specs=pl.BlockSpec((1, HEAD_W, HEAD_W), lambda h: (h, 0, 0)),
        out_shape=jax.ShapeDtypeStruct((HEADS, HEAD_W, HEAD_W), F32),
        compiler_params=_params(("parallel",), 1 << 20),
        name="hgrn2_meta_state",
    )(proj_meta, proj_meta, _chunk_tri(HG_CHUNK), lower_bound)


def _hgrn_kernel(q_ref, f_ref, i_ref, g_ref, tri_ref, lb_ref, s0_ref, on_ref, o_ref, st_ref):
    @pl.when(pl.program_id(2) == 0)
    def _():
        st_ref[...] = s0_ref[0]

    outs, st = _hg_block(q_ref[...].astype(F32), f_ref[...].astype(F32), i_ref[...],
                         _lower_bound(lb_ref), tri_ref[...], st_ref[...])
    st_ref[...] = st
    for i, o in enumerate(outs):
        sl = pl.ds(i * HG_CHUNK, HG_CHUNK)
        r = lax.rsqrt(jnp.mean(o * o, axis=-1, keepdims=True) + EPS)
        y = ((o * r) * on_ref[...]) * jax.nn.sigmoid(g_ref[sl, :].astype(F32))
        o_ref[sl, :] = y.astype(o_ref.dtype)


def _hgrn2(proj, s0, lower_bound, out_norm, batch, seq, tb):
    nt = seq // tb

    def col(off):
        return pl.BlockSpec((tb, HEAD_W), lambda b, h, t: (b * nt + t, off + h))

    return pl.pallas_call(
        _hgrn_kernel,
        grid=(batch, HEADS, nt),
        in_specs=[col(COL_HG_Q), col(COL_HG_F), col(COL_HG_I), col(COL_HG_G),
                  pl.BlockSpec((tb, tb), lambda b, h, t: (0, 0)),
                  pl.BlockSpec((lower_bound.shape[0], HEAD_W), lambda b, h, t: (0, h)),
                  pl.BlockSpec((1, HEAD_W, HEAD_W), lambda b, h, t: (h, 0, 0)),
                  pl.BlockSpec((1, HEAD_W), lambda b, h, t: (0, 0))],
        out_specs=pl.BlockSpec((tb, HEAD_W), lambda b, h, t: (b * nt + t, h)),
        out_shape=jax.ShapeDtypeStruct((batch * seq, HEADS * HEAD_W), BF16),
        scratch_shapes=[pltpu.VMEM((HEAD_W, HEAD_W), F32)],
        compiler_params=_params(("parallel", "parallel", "arbitrary"),
                                5 * tb * HEAD_W * 2 + tb * tb * 2, 24 * tb * HEAD_W * 4),
        name="hgrn2",
    )(proj, proj, proj, proj, _chunk_tri(tb), lower_bound, s0, out_norm)


def _merge_kernel(ya_ref, yb_ref, wa_ref, wb_ref, ga_ref, gb_ref, o_ref):
    a = jnp.dot(ya_ref[...], wa_ref[...], preferred_element_type=F32)
    b = jnp.dot(yb_ref[...], wb_ref[...], preferred_element_type=F32)
    o = (jax.nn.sigmoid(ga_ref[...].astype(F32)) * a + jax.nn.sigmoid(gb_ref[...].astype(F32)) * b)
    o_ref[...] = o.astype(o_ref.dtype)


def _gated_merge(y_a, y_b, w_a, w_b, proj, tm, tn):
    m, kdim = y_a.shape
    n = w_a.shape[1]
    tm = min(tm, m)
    blk = 2 * tm * kdim * 2 + 2 * kdim * tn * 2 + 3 * tm * tn * 2
    return pl.pallas_call(
        _merge_kernel,
        grid=(m // tm, n // tn),
        in_specs=[pl.BlockSpec((tm, kdim), lambda i, j: (i, 0)),
                  pl.BlockSpec((tm, kdim), lambda i, j: (i, 0)),
                  pl.BlockSpec((kdim, tn), lambda i, j: (0, j)),
                  pl.BlockSpec((kdim, tn), lambda i, j: (0, j)),
                  pl.BlockSpec((tm, tn), lambda i, j: (i, COL_GATE_A // tn + j)),
                  pl.BlockSpec((tm, tn), lambda i, j: (i, COL_GATE_B // tn + j))],
        out_specs=pl.BlockSpec((tm, tn), lambda i, j: (i, j)),
        out_shape=jax.ShapeDtypeStruct((m, n), BF16),
        compiler_params=_params(("parallel", "parallel"), blk, 4 * tm * tn * 4),
        name="gated_merge",
    )(y_a, y_b, w_a, w_b, proj, proj)


def kernel(x, meta_tokens, norm_mix, w_in, da_q_norm, da_k_norm, da_lambda_q1, da_lambda_k1,
           da_lambda_q2, da_lambda_k2, da_subln, hg_lower_bound, hg_out_norm, w_up_a, w_up_b,
           w_out, norm_mlp, w_ff1, w_ff2):
    batch, seq, d = x.shape
    assert d == D_MODEL and w_in.shape[0] == 1 and w_in.shape[2] == IN_WIDTH
    assert meta_tokens.shape == (N_META, D_MODEL) and seq % 512 == 0
    m = batch * seq
    xf = x.reshape(m, d)
    meta_pad = jnp.concatenate([jnp.zeros((META_ROWS - N_META, d), F32), meta_tokens.astype(F32)])

    w_in_b = w_in[0].astype(BF16)
    w_a_b, w_b_b = w_up_a[0].astype(BF16), w_up_b[0].astype(BF16)
    w_out_b, w_ff1_b, w_ff2_b = w_out[0].astype(BF16), w_ff1[0].astype(BF16), w_ff2[0].astype(BF16)

    u = _rmsnorm(xf, norm_mix[0], 256)
    u_meta = _rmsnorm(meta_pad, norm_mix[0], META_ROWS)
    proj = _matmul(u, w_in_b, tm=1024, tn=1024, tk=d, out_dtype=BF16, name="in_proj")
    proj_meta = _matmul(u_meta, w_in_b, tm=META_ROWS, tn=1024, tk=d, out_dtype=BF16,
                        name="in_proj_meta")

    gains = jnp.stack([jnp.tile(da_q_norm[0], 2) * QK_DIM ** -0.5, jnp.tile(da_k_norm[0], 2)])
    pos_meta = jnp.maximum(jnp.arange(META_ROWS, dtype=jnp.int32) - (META_ROWS - N_META), 0)
    pos_main = jnp.arange(seq, dtype=jnp.int32) + N_META
    qk = _qk_prep(proj, _rope_tables(pos_main), gains, 512, seq, 4)
    qk_meta = _qk_prep(proj_meta, _rope_tables(pos_meta), gains, META_ROWS, META_ROWS, 4)
    lam_params = jnp.stack([da_lambda_q1[0], da_lambda_k1[0], da_lambda_q2[0], da_lambda_k2[0]])
    y_a = _attention(qk, proj, qk_meta, proj_meta, lam_params, da_subln[0].reshape(1, HEAD_W),
                     batch, seq, 512)

    s0 = _hg_meta_state(proj_meta, hg_lower_bound)
    y_b = _hgrn2(proj, s0, hg_lower_bound, hg_out_norm[0].reshape(1, HEAD_W), batch, seq, 512)

    merged = _gated_merge(y_a, y_b, w_a_b, w_b_b, proj, 1024, 512)
    h1 = _matmul(merged, w_out_b, tm=1024, tn=1024, tk=d, out_dtype=F32, res=xf, name="out_proj")

    v = _rmsnorm(h1, norm_mlp[0], 256)
    hid = _matmul(v, w_ff1_b, tm=1024, tn=1024, tk=d, out_dtype=BF16, relu2=True, name="ff1")
    out = _matmul(hid, w_ff2_b, tm=1024, tn=1024, tk=2048, out_dtype=F32, res=h1, name="ff2")
    return out.reshape(batch, seq, d)
```

```python
import functools
import math

import jax
import jax.numpy as jnp
from jax import lax
from jax.experimental import pallas as pl
from jax.experimental.pallas import tpu as pltpu

F32 = jnp.float32
BF16 = jnp.bfloat16

D_MODEL = 4096
N_META = 16
HEADS = 16
HEAD_W = 128
QK_DIM = 64
ROPE_DIMS = 16
ROPE_THETA = 500000.0
HG_CHUNK = 64
META_ROWS = 128
EPS = 1e-6
NEG_INF = -1e30
LAM_INIT = 0.8 - 0.6 * math.exp(-0.3 * 0)

COL_DA_Q, COL_DA_K, COL_DA_V = 0, 16, 32
COL_HG_Q, COL_HG_F, COL_HG_I, COL_HG_G = 48, 64, 80, 96
COL_GATE_A, COL_GATE_B = 112 * HEAD_W, 144 * HEAD_W
IN_WIDTH = 176 * HEAD_W
META_COL_K, META_COL_V, META_COL_F, META_COL_I = 0, 16, 32, 48
META_WIDTH = 64 * HEAD_W

V7X_VMEM_LIMIT_CAP = 58 * 2**20


def _params(semantics, block_bytes, extra_bytes=0):
    need = 2 * block_bytes + extra_bytes + (4 << 20)
    return pltpu.CompilerParams(dimension_semantics=semantics,
                                vmem_limit_bytes=min(max(need, 16 << 20), V7X_VMEM_LIMIT_CAP))


def _rmsnorm_kernel(x_ref, g_ref, o_ref):
    x = x_ref[...]
    r = lax.rsqrt(jnp.mean(x * x, axis=-1, keepdims=True) + EPS)
    o_ref[...] = ((x * r) * g_ref[...]).astype(o_ref.dtype)


def _rmsnorm(x, g, tm):
    m, d = x.shape
    return pl.pallas_call(
        _rmsnorm_kernel,
        grid=(m // tm,),
        in_specs=[pl.BlockSpec((tm, d), lambda i: (i, 0)),
                  pl.BlockSpec((1, d), lambda i: (0, 0))],
        out_specs=pl.BlockSpec((tm, d), lambda i: (i, 0)),
        out_shape=jax.ShapeDtypeStruct((m, d), BF16),
        compiler_params=_params(("parallel",), tm * d * 6),
        name="rmsnorm",
    )(x, g.reshape(1, d))


def _mm_kernel(*refs, nk, relu2, has_res):
    a_ref, w_ref = refs[0], refs[1]
    res_ref = refs[2] if has_res else None
    o_ref = refs[2 + has_res]
    part = jnp.dot(a_ref[...], w_ref[...], preferred_element_type=F32)

    def finish(acc):
        if relu2:
            acc = jnp.square(jnp.maximum(acc, 0.0))
        if has_res:
            acc = acc + res_ref[...]
        o_ref[...] = acc.astype(o_ref.dtype)

    if nk == 1:
        finish(part)
        return
    acc_ref = refs[3 + has_res]
    k = pl.program_id(2)

    @pl.when(k == 0)
    def _():
        acc_ref[...] = part

    @pl.when(k > 0)
    def _():
        acc_ref[...] += part

    @pl.when(k == nk - 1)
    def _():
        finish(acc_ref[...])


def _matmul(a, w, *, tm, tn, tk, out_dtype, relu2=False, res=None, name):
    m, kdim = a.shape
    n = w.shape[1]
    tm = min(tm, m)
    nk = kdim // tk
    in_specs = [pl.BlockSpec((tm, tk), lambda i, j, k: (i, k)),
                pl.BlockSpec((tk, tn), lambda i, j, k: (k, j))]
    args = [a, w]
    out_bytes = jnp.dtype(out_dtype).itemsize
    block_bytes = tm * tk * 2 + tk * tn * 2 + tm * tn * out_bytes
    if res is not None:
        in_specs.append(pl.BlockSpec((tm, tn), lambda i, j, k: (i, j)))
        args.append(res)
        block_bytes += tm * tn * 4
    scratch = [pltpu.VMEM((tm, tn), F32)] if nk > 1 else []
    return pl.pallas_call(
        functools.partial(_mm_kernel, nk=nk, relu2=relu2, has_res=res is not None),
        grid=(m // tm, n // tn, nk),
        in_specs=in_specs,
        out_specs=pl.BlockSpec((tm, tn), lambda i, j, k: (i, j)),
        out_shape=jax.ShapeDtypeStruct((m, n), out_dtype),
        scratch_shapes=scratch,
        compiler_params=_params(("parallel", "parallel", "arbitrary"), block_bytes,
                                2 * tm * tn * 4),
        name=name,
    )(*args)


def _mm_wcast_kernel(*refs, relu2, has_res):
    a_ref, w_ref = refs[0], refs[1]
    res_ref = refs[2] if has_res else None
    o_ref, wb_ref = refs[2 + has_res], refs[3 + has_res]

    @pl.when(pl.program_id(1) == 0)
    def _():
        wb_ref[...] = w_ref[...].astype(BF16)

    acc = jnp.dot(a_ref[...], wb_ref[...], preferred_element_type=F32)
    if relu2:
        acc = jnp.square(jnp.maximum(acc, 0.0))
    if has_res:
        acc = acc + res_ref[...]
    o_ref[...] = acc.astype(o_ref.dtype)


def _matmul_wcast(a, w, *, tm, tn, out_dtype, relu2=False, res=None, name, w_cols=None):
    m, kdim = a.shape
    n, w_col = (w.shape[1], lambda j: j) if w_cols is None else w_cols
    tm = min(tm, m)
    in_specs = [pl.BlockSpec((tm, kdim), lambda j, i: (i, 0)),
                pl.BlockSpec((kdim, tn), lambda j, i: (0, w_col(j)))]
    args = [a, w]
    out_bytes = jnp.dtype(out_dtype).itemsize
    block_bytes = tm * kdim * 2 + kdim * tn * 4 + tm * tn * out_bytes
    if res is not None:
        in_specs.append(pl.BlockSpec((tm, tn), lambda j, i: (i, j)))
        args.append(res)
        block_bytes += tm * tn * 4
    return pl.pallas_call(
        functools.partial(_mm_wcast_kernel, relu2=relu2, has_res=res is not None),
        grid=(n // tn, m // tm),
        in_specs=in_specs,
        out_specs=pl.BlockSpec((tm, tn), lambda j, i: (i, j)),
        out_shape=jax.ShapeDtypeStruct((m, n), out_dtype),
        scratch_shapes=[pltpu.VMEM((kdim, tn), BF16)],
        compiler_params=_params(("arbitrary", "arbitrary"), block_bytes,
                                kdim * tn * 2 + 2 * tm * tn * 4),
        name=name,
    )(*args)


def _split_terms(x, n):
    terms = []
    for _ in range(n - 1):
        t = x.astype(BF16)
        terms.append(t)
        x = x - t.astype(F32)
    terms.append(x.astype(BF16))
    return terms


def _qk_prep_kernel(x_ref, cos_ref, sin_lo_ref, sin_hi_ref, g_ref, ones_ref, o_ref, *,
                    heads_per_block, q_blocks):
    g = jnp.where(pl.program_id(1) < q_blocks, g_ref[0:1, :], g_ref[1:2, :])
    cos, sin_lo, sin_hi = cos_ref[...], sin_lo_ref[...], sin_hi_ref[...]
    for hh in range(heads_per_block):
        sl = slice(hh * HEAD_W, (hh + 1) * HEAD_W)
        x = x_ref[:, sl].astype(F32)
        sq = jnp.concatenate(_split_terms(x * x, 2), axis=1)
        ss = jnp.dot(sq, ones_ref[...], preferred_element_type=F32)
        r = lax.rsqrt(ss * (1.0 / QK_DIM) + EPS)
        y = (x * r) * g
        y = (y * cos + pltpu.roll(y, HEAD_W - ROPE_DIMS // 2, 1) * sin_lo
             + pltpu.roll(y, ROPE_DIMS // 2, 1) * sin_hi)
        o_ref[:, sl] = y.astype(o_ref.dtype)


def _qk_prep(proj, tables, gains, tm, rows_per_seq, heads_per_block, q_heads, k_heads):
    m = proj.shape[0]
    nblk = rows_per_seq // tm
    tw = heads_per_block * HEAD_W
    ncol = (q_heads + k_heads) // heads_per_block
    lane_group = jnp.arange(HEAD_W) // QK_DIM
    group_ones = jnp.tile((lane_group[:, None] == lane_group[None, :]).astype(BF16), (2, 1))
    tab_spec = pl.BlockSpec((tm, HEAD_W), lambda i, j: (i % nblk, 0))
    return pl.pallas_call(
        functools.partial(_qk_prep_kernel, heads_per_block=heads_per_block,
                          q_blocks=q_heads // heads_per_block),
        grid=(m // tm, ncol),
        in_specs=[pl.BlockSpec((tm, tw), lambda i, j: (i, j)),
                  tab_spec, tab_spec, tab_spec,
                  pl.BlockSpec((2, HEAD_W), lambda i, j: (0, 0)),
                  pl.BlockSpec((2 * HEAD_W, HEAD_W), lambda i, j: (0, 0))],
        out_specs=pl.BlockSpec((tm, tw), lambda i, j: (i, j)),
        out_shape=jax.ShapeDtypeStruct((m, ncol * tw), BF16),
        compiler_params=_params(("parallel", "parallel"), tm * tw * 4 + 3 * tm * HEAD_W * 4,
                                tm * HEAD_W * 48),
        name="qk_prep",
    )(proj, *tables, gains, group_ones)


def _rope_tables(pos):
    half = ROPE_DIMS // 2
    inv_freq = ROPE_THETA ** (-(jnp.arange(half, dtype=F32) * 2.0) / ROPE_DIMS)
    ang = pos.astype(F32)[:, None] * inv_freq[None, :]
    cos, sin = jnp.cos(ang), jnp.sin(ang)
    n = pos.shape[0]
    pad = QK_DIM - ROPE_DIMS
    ones, zeros, zh = jnp.ones((n, pad), F32), jnp.zeros((n, pad), F32), jnp.zeros((n, half), F32)
    cos_t = jnp.concatenate([cos, cos, ones], axis=1)
    sin_lo = jnp.concatenate([-sin, zh, zeros], axis=1)
    sin_hi = jnp.concatenate([zh, sin, zeros], axis=1)
    return tuple(jnp.tile(t, (1, 2)) for t in (cos_t, sin_lo, sin_hi))


def _nt_dot(a, b):
    return lax.dot_general(a, b, (((1,), (1,)), ((), ())), preferred_element_type=F32)


def _softmax_step(s, vb, m, l, acc):
    m_new = jnp.maximum(m, jnp.max(s, axis=-1, keepdims=True))
    alpha = jnp.exp(m - m_new)
    p = jnp.exp(s - m_new)
    l = alpha * l + jnp.sum(p, axis=-1, keepdims=True)
    acc = alpha * acc + jnp.dot(p.astype(BF16), vb, preferred_element_type=F32)
    return m_new, l, acc


def _attn_kernel(q_ref, k_ref, v_ref, km_ref, vm_ref, lam_ref, subln_ref, o_ref, *, tq):
    qi = pl.program_id(2)
    q = q_ref[...]
    lane = lax.broadcasted_iota(jnp.int32, q.shape, 1)
    zero = jnp.zeros_like(q)
    q1 = jnp.where(lane < QK_DIM, q, zero)
    q2 = jnp.where(lane >= QK_DIM, q, zero)

    def step(carry, kb, vb, mask):
        m1, l1, a1, m2, l2, a2 = carry
        s1, s2 = _nt_dot(q1, kb), _nt_dot(q2, kb)
        if mask is not None:
            s1 = jnp.where(mask, s1, NEG_INF)
            s2 = jnp.where(mask, s2, NEG_INF)
        m1, l1, a1 = _softmax_step(s1, vb, m1, l1, a1)
        m2, l2, a2 = _softmax_step(s2, vb, m2, l2, a2)
        return m1, l1, a1, m2, l2, a2

    col_m = lax.broadcasted_iota(jnp.int32, (tq, META_ROWS), 1)
    m0 = jnp.full((tq, 1), NEG_INF, F32)
    l0 = jnp.zeros((tq, 1), F32)
    a0 = jnp.zeros((tq, HEAD_W), F32)
    carry = step((m0, l0, a0, m0, l0, a0), km_ref[...], vm_ref[...],
                 col_m >= META_ROWS - N_META)

    def body(j, c):
        off = pl.multiple_of(j * tq, tq)
        return step(c, k_ref[pl.ds(off, tq), :], v_ref[pl.ds(off, tq), :], None)

    carry = lax.fori_loop(0, qi, body, carry)
    off = pl.multiple_of(qi * tq, tq)
    row = lax.broadcasted_iota(jnp.int32, (tq, tq), 0)
    col = lax.broadcasted_iota(jnp.int32, (tq, tq), 1)
    m1, l1, a1, m2, l2, a2 = step(carry, k_ref[pl.ds(off, tq), :], v_ref[pl.ds(off, tq), :],
                                  col <= row)

    lp = lam_ref[...]
    lam = (jnp.exp(jnp.sum(lp[0:1] * lp[1:2], axis=-1, keepdims=True))
           - jnp.exp(jnp.sum(lp[2:3] * lp[3:4], axis=-1, keepdims=True)) + LAM_INIT)
    o = a1 / l1 - lam * (a2 / l2)
    r = lax.rsqrt(jnp.mean(o * o, axis=-1, keepdims=True) + EPS)
    o_ref[...] = (((o * r) * subln_ref[...]) * (1.0 - LAM_INIT)).astype(o_ref.dtype)


def _attention(qk, proj, qk_meta, proj_meta, lam_params, subln, batch, seq, tq):
    nq = seq // tq
    kernel = functools.partial(_attn_kernel, tq=tq)
    blk = (tq * HEAD_W * 2 + 2 * seq * HEAD_W * 2 + 2 * META_ROWS * HEAD_W * 2 + tq * HEAD_W * 2)
    return pl.pallas_call(
        kernel,
        grid=(batch, HEADS, nq),
        in_specs=[pl.BlockSpec((tq, HEAD_W), lambda b, h, i: (b * nq + i, h)),
                  pl.BlockSpec((seq, HEAD_W), lambda b, h, i: (b, HEADS + h)),
                  pl.BlockSpec((seq, HEAD_W), lambda b, h, i: (b, COL_DA_V + h)),
                  pl.BlockSpec((META_ROWS, HEAD_W), lambda b, h, i: (0, h)),
                  pl.BlockSpec((META_ROWS, HEAD_W), lambda b, h, i: (0, META_COL_V + h)),
                  pl.BlockSpec((4, QK_DIM), lambda b, h, i: (0, 0)),
                  pl.BlockSpec((1, HEAD_W), lambda b, h, i: (0, 0))],
        out_specs=pl.BlockSpec((tq, HEAD_W), lambda b, h, i: (b * nq + i, h)),
        out_shape=jax.ShapeDtypeStruct((batch * seq, HEADS * HEAD_W), BF16),
        compiler_params=_params(("parallel", "parallel", "arbitrary"), blk, 8 * tq * tq * 4),
        name="diff_attention",
    )(qk, qk, proj, qk_meta, proj_meta, lam_params, subln)


def _lower_bound(lb_ref):
    x = lb_ref[...]
    e = jnp.exp(x - jnp.max(x, axis=0, keepdims=True))
    return e[0:1] / jnp.sum(e, axis=0, keepdims=True)


def _chunk_tri(rows):
    r = jnp.arange(rows)
    same_chunk = r[:, None] // HG_CHUNK == r[None, :] // HG_CHUNK
    return (same_chunk & (r[None, :] <= r[:, None])).astype(BF16)


def _chunk_cumsum(tri, g):
    parts = jnp.concatenate(_split_terms(g, 3), axis=1)
    s = jnp.dot(tri, parts, preferred_element_type=F32)
    return s[:, :HEAD_W] + s[:, HEAD_W:2 * HEAD_W] + s[:, 2 * HEAD_W:]


def _hg_block(q, fl, v, lb, tri, st, valid=None):
    c = HG_CHUNK
    n = q.shape[0] // c
    f = lb + (1.0 - lb) * jax.nn.sigmoid(fl)
    g = jnp.log(f)
    kk = 1.0 - f
    if valid is not None:
        g = jnp.where(valid, g, 0.0)
        kk = jnp.where(valid, kk, 0.0)
        v = jnp.where(valid, v, jnp.zeros_like(v))
    b = _chunk_cumsum(tri, g).reshape(n, c, HEAD_W)
    q3, k3, v3 = q.reshape(n, c, HEAD_W), kk.reshape(n, c, HEAD_W), v.reshape(n, c, HEAD_W)
    b_mid = b[:, c // 2 - 1:c // 2, :]
    b_last = b[:, c - 1:c, :]
    q_in = (q3 * jnp.exp(b - b_mid)).astype(BF16)
    k_in = (k3 * jnp.exp(b_mid - b)).astype(BF16)
    k_dec = (k3 * jnp.exp(b_last - b)).astype(BF16)
    q_ex = (q3 * jnp.exp(b)).astype(BF16)
    decay = jnp.exp(b_last)
    row = lax.broadcasted_iota(jnp.int32, (n, c, c), 1)
    col = lax.broadcasted_iota(jnp.int32, (n, c, c), 2)
    a = lax.dot_general(q_in, k_in, (((2,), (2,)), ((0,), (0,))), preferred_element_type=F32)
    a = jnp.where(col <= row, a, 0.0).astype(BF16)
    o_intra = lax.dot_general(a, v3, (((2,), (1,)), ((0,), (0,))), preferred_element_type=F32)
    u_t = [lax.dot_general(v3[i], k_dec[i], (((0,), (0,)), ((), ())), preferred_element_type=F32)
           for i in range(n)]
    outs = []
    for i in range(n):
        outs.append(o_intra[i] + _nt_dot(q_ex[i], st.astype(BF16)))
        st = st * decay[i] + u_t[i]
    return outs, st


def _hg_meta_kernel(f_ref, i_ref, tri_ref, lb_ref, s_ref):
    row = lax.broadcasted_iota(jnp.int32, (HG_CHUNK, HEAD_W), 0)
    _, st = _hg_block(jnp.zeros((HG_CHUNK, HEAD_W), F32), f_ref[...].astype(F32), i_ref[...],
                      _lower_bound(lb_ref), tri_ref[...], jnp.zeros((HEAD_W, HEAD_W), F32),
                      valid=row >= HG_CHUNK - N_META)
    s_ref[0] = st


def _hg_meta_state(proj_meta, lower_bound):
    blk = META_ROWS // HG_CHUNK - 1
    return pl.pallas_call(
        _hg_meta_kernel,
        grid=(HEADS,),
        in_specs=[pl.BlockSpec((HG_CHUNK, HEAD_W), lambda h: (blk, META_COL_F + h)),
                  pl.BlockSpec((HG_CHUNK, HEAD_W), lambda h: (blk, META_COL_I + h)),
                  pl.BlockSpec((HG_CHUNK, HG_CHUNK), lambda h: (0, 0)),
                  pl.BlockSpec((lower_bound.shape[0], HEAD_W), lambda h: (0, h))],
        out_specs=pl.BlockSpec((1, HEAD_W, HEAD_W), lambda h: (h, 0, 0)),
        out_shape=jax.ShapeDtypeStruct((HEADS, HEAD_W, HEAD_W), F32),
        compiler_params=_params(("parallel",), 1 << 20),
        name="hgrn2_meta_state",
    )(proj_meta, proj_meta, _chunk_tri(HG_CHUNK), lower_bound)


def _hgrn_kernel(q_ref, f_ref, i_ref, g_ref, tri_ref, lb_ref, s0_ref, on_ref, o_ref, st_ref):
    @pl.when(pl.program_id(2) == 0)
    def _():
        st_ref[...] = s0_ref[0]

    outs, st = _hg_block(q_ref[...].astype(F32), f_ref[...].astype(F32), i_ref[...],
                         _lower_bound(lb_ref), tri_ref[...], st_ref[...])
    st_ref[...] = st
    for i, o in enumerate(outs):
        sl = pl.ds(i * HG_CHUNK, HG_CHUNK)
        r = lax.rsqrt(jnp.mean(o * o, axis=-1, keepdims=True) + EPS)
        y = ((o * r) * on_ref[...]) * jax.nn.sigmoid(g_ref[sl, :].astype(F32))
        o_ref[sl, :] = y.astype(o_ref.dtype)


def _hgrn2(proj, s0, lower_bound, out_norm, batch, seq, tb):
    nt = seq // tb

    def col(off):
        return pl.BlockSpec((tb, HEAD_W), lambda b, h, t: (b * nt + t, off + h))

    return pl.pallas_call(
        _hgrn_kernel,
        grid=(batch, HEADS, nt),
        in_specs=[col(COL_HG_Q), col(COL_HG_F), col(COL_HG_I), col(COL_HG_G),
                  pl.BlockSpec((tb, tb), lambda b, h, t: (0, 0)),
                  pl.BlockSpec((lower_bound.shape[0], HEAD_W), lambda b, h, t: (0, h)),
                  pl.BlockSpec((1, HEAD_W, HEAD_W), lambda b, h, t: (h, 0, 0)),
                  pl.BlockSpec((1, HEAD_W), lambda b, h, t: (0, 0))],
        out_specs=pl.BlockSpec((tb, HEAD_W), lambda b, h, t: (b * nt + t, h)),
        out_shape=jax.ShapeDtypeStruct((batch * seq, HEADS * HEAD_W), BF16),
        scratch_shapes=[pltpu.VMEM((HEAD_W, HEAD_W), F32)],
        compiler_params=_params(("parallel", "parallel", "arbitrary"),
                                5 * tb * HEAD_W * 2 + tb * tb * 2, 24 * tb * HEAD_W * 4),
        name="hgrn2",
    )(proj, proj, proj, proj, _chunk_tri(tb), lower_bound, s0, out_norm)


def _merge_kernel(ya_ref, yb_ref, wa_ref, wb_ref, ga_ref, gb_ref, o_ref, wa_bf_ref, wb_bf_ref):
    @pl.when(pl.program_id(1) == 0)
    def _():
        wa_bf_ref[...] = wa_ref[...].astype(BF16)
        wb_bf_ref[...] = wb_ref[...].astype(BF16)

    a = jnp.dot(ya_ref[...], wa_bf_ref[...], preferred_element_type=F32)
    b = jnp.dot(yb_ref[...], wb_bf_ref[...], preferred_element_type=F32)
    o = (jax.nn.sigmoid(ga_ref[...].astype(F32)) * a + jax.nn.sigmoid(gb_ref[...].astype(F32)) * b)
    o_ref[...] = o.astype(o_ref.dtype)


def _gated_merge(y_a, y_b, w_a, w_b, proj, tm, tn):
    m, kdim = y_a.shape
    n = w_a.shape[1]
    tm = min(tm, m)
    blk = 2 * tm * kdim * 2 + 2 * kdim * tn * 4 + 3 * tm * tn * 2
    return pl.pallas_call(
        _merge_kernel,
        grid=(n // tn, m // tm),
        in_specs=[pl.BlockSpec((tm, kdim), lambda j, i: (i, 0)),
                  pl.BlockSpec((tm, kdim), lambda j, i: (i, 0)),
                  pl.BlockSpec((kdim, tn), lambda j, i: (0, j)),
                  pl.BlockSpec((kdim, tn), lambda j, i: (0, j)),
                  pl.BlockSpec((tm, tn), lambda j, i: (i, COL_GATE_A // tn + j)),
                  pl.BlockSpec((tm, tn), lambda j, i: (i, COL_GATE_B // tn + j))],
        out_specs=pl.BlockSpec((tm, tn), lambda j, i: (i, j)),
        out_shape=jax.ShapeDtypeStruct((m, n), BF16),
        scratch_shapes=[pltpu.VMEM((kdim, tn), BF16), pltpu.VMEM((kdim, tn), BF16)],
        compiler_params=_params(("arbitrary", "arbitrary"), blk,
                                2 * kdim * tn * 2 + 4 * tm * tn * 4),
        name="gated_merge",
    )(y_a, y_b, w_a, w_b, proj, proj)


def kernel(x, meta_tokens, norm_mix, w_in, da_q_norm, da_k_norm, da_lambda_q1, da_lambda_k1,
           da_lambda_q2, da_lambda_k2, da_subln, hg_lower_bound, hg_out_norm, w_up_a, w_up_b,
           w_out, norm_mlp, w_ff1, w_ff2):
    batch, seq, d = x.shape
    assert d == D_MODEL and w_in.shape[0] == 1 and w_in.shape[2] == IN_WIDTH
    assert meta_tokens.shape == (N_META, D_MODEL) and seq % 512 == 0
    m = batch * seq
    xf = x.reshape(m, d)
    meta_pad = jnp.concatenate([jnp.zeros((META_ROWS - N_META, d), F32), meta_tokens.astype(F32)])

    tn = 512
    u = _rmsnorm(xf, norm_mix[0], 256)
    u_meta = _rmsnorm(meta_pad, norm_mix[0], META_ROWS)
    proj = _matmul_wcast(u, w_in[0], tm=1024, tn=tn, out_dtype=BF16, name="in_proj")
    kv_blocks = 2 * HEADS * HEAD_W // tn

    def meta_col(j):
        return jnp.where(j < kv_blocks, COL_DA_K * HEAD_W // tn + j,
                         COL_HG_F * HEAD_W // tn + j - kv_blocks)

    proj_meta = _matmul_wcast(u_meta, w_in[0], tm=META_ROWS, tn=tn, out_dtype=BF16,
                              name="in_proj_meta", w_cols=(META_WIDTH, meta_col))

    gains = jnp.stack([jnp.tile(da_q_norm[0], 2) * QK_DIM ** -0.5, jnp.tile(da_k_norm[0], 2)])
    pos_meta = jnp.maximum(jnp.arange(META_ROWS, dtype=jnp.int32) - (META_ROWS - N_META), 0)
    pos_main = jnp.arange(seq, dtype=jnp.int32) + N_META
    qk = _qk_prep(proj, _rope_tables(pos_main), gains, 512, seq, 4, HEADS, HEADS)
    qk_meta = _qk_prep(proj_meta, _rope_tables(pos_meta), gains, META_ROWS, META_ROWS, 4, 0, HEADS)
    lam_params = jnp.stack([da_lambda_q1[0], da_lambda_k1[0], da_lambda_q2[0], da_lambda_k2[0]])
    y_a = _attention(qk, proj, qk_meta, proj_meta, lam_params, da_subln[0].reshape(1, HEAD_W),
                     batch, seq, 512)

    s0 = _hg_meta_state(proj_meta, hg_lower_bound)
    y_b = _hgrn2(proj, s0, hg_lower_bound, hg_out_norm[0].reshape(1, HEAD_W), batch, seq, 512)

    merged = _gated_merge(y_a, y_b, w_up_a[0], w_up_b[0], proj, 1024, tn)
    h1 = _matmul_wcast(merged, w_out[0], tm=1024, tn=tn, out_dtype=F32, res=xf, name="out_proj")

    v = _rmsnorm(h1, norm_mlp[0], 256)
    hid = _matmul_wcast(v, w_ff1[0], tm=1024, tn=tn, out_dtype=BF16, relu2=True, name="ff1")
    out = _matmul(hid, w_ff2[0].astype(BF16), tm=1024, tn=1024, tk=2048, out_dtype=F32, res=h1,
                  name="ff2")
    return out.reshape(batch, seq, d)
```

```python
import functools
import math

import jax
import jax.numpy as jnp
from jax import lax
from jax.experimental import pallas as pl
from jax.experimental.pallas import tpu as pltpu

F32 = jnp.float32
BF16 = jnp.bfloat16

D_MODEL = 4096
N_META = 16
HEADS = 16
HEAD_W = 128
QK_DIM = 64
ROPE_DIMS = 16
ROPE_THETA = 500000.0
HG_CHUNK = 64
META_ROWS = 128
ATT_BLOCK = 512
EPS = 1e-6
NEG_INF = -1e30
LAM_INIT = 0.8 - 0.6 * math.exp(-0.3 * 0)

COL_DA_Q, COL_DA_K, COL_DA_V = 0, 16, 32
COL_HG_Q, COL_HG_F, COL_HG_I, COL_HG_G = 48, 64, 80, 96
COL_GATE_A, COL_GATE_B = 112 * HEAD_W, 144 * HEAD_W
IN_WIDTH = 176 * HEAD_W
META_COL_K, META_COL_V, META_COL_F, META_COL_I = 0, 16, 32, 48
META_WIDTH = 64 * HEAD_W

V7X_VMEM_LIMIT_CAP = 58 * 2**20


def _params(semantics, block_bytes, extra_bytes=0):
    need = 2 * block_bytes + extra_bytes + (4 << 20)
    return pltpu.CompilerParams(dimension_semantics=semantics,
                                vmem_limit_bytes=min(max(need, 16 << 20), V7X_VMEM_LIMIT_CAP))


def _rmsnorm_kernel(x_ref, g_ref, o_ref):
    x = x_ref[...]
    r = lax.rsqrt(jnp.mean(x * x, axis=-1, keepdims=True) + EPS)
    o_ref[...] = ((x * r) * g_ref[...]).astype(o_ref.dtype)


def _rmsnorm(x, g, tm):
    m, d = x.shape
    return pl.pallas_call(
        _rmsnorm_kernel,
        grid=(m // tm,),
        in_specs=[pl.BlockSpec((tm, d), lambda i: (i, 0)),
                  pl.BlockSpec((1, d), lambda i: (0, 0))],
        out_specs=pl.BlockSpec((tm, d), lambda i: (i, 0)),
        out_shape=jax.ShapeDtypeStruct((m, d), BF16),
        compiler_params=_params(("parallel",), tm * d * 6),
        name="rmsnorm",
    )(x, g.reshape(1, d))


def _mm_kernel(*refs, nk, relu2, has_res):
    a_ref, w_ref = refs[0], refs[1]
    res_ref = refs[2] if has_res else None
    o_ref = refs[2 + has_res]
    part = jnp.dot(a_ref[...], w_ref[...], preferred_element_type=F32)

    def finish(acc):
        if relu2:
            acc = jnp.square(jnp.maximum(acc, 0.0))
        if has_res:
            acc = acc + res_ref[...]
        o_ref[...] = acc.astype(o_ref.dtype)

    if nk == 1:
        finish(part)
        return
    acc_ref = refs[3 + has_res]
    k = pl.program_id(2)

    @pl.when(k == 0)
    def _():
        acc_ref[...] = part

    @pl.when(k > 0)
    def _():
        acc_ref[...] += part

    @pl.when(k == nk - 1)
    def _():
        finish(acc_ref[...])


def _matmul(a, w, *, tm, tn, tk, out_dtype, relu2=False, res=None, name):
    m, kdim = a.shape
    n = w.shape[1]
    tm = min(tm, m)
    nk = kdim // tk
    in_specs = [pl.BlockSpec((tm, tk), lambda i, j, k: (i, k)),
                pl.BlockSpec((tk, tn), lambda i, j, k: (k, j))]
    args = [a, w]
    out_bytes = jnp.dtype(out_dtype).itemsize
    block_bytes = tm * tk * 2 + tk * tn * 2 + tm * tn * out_bytes
    if res is not None:
        in_specs.append(pl.BlockSpec((tm, tn), lambda i, j, k: (i, j)))
        args.append(res)
        block_bytes += tm * tn * 4
    scratch = [pltpu.VMEM((tm, tn), F32)] if nk > 1 else []
    return pl.pallas_call(
        functools.partial(_mm_kernel, nk=nk, relu2=relu2, has_res=res is not None),
        grid=(m // tm, n // tn, nk),
        in_specs=in_specs,
        out_specs=pl.BlockSpec((tm, tn), lambda i, j, k: (i, j)),
        out_shape=jax.ShapeDtypeStruct((m, n), out_dtype),
        scratch_shapes=scratch,
        compiler_params=_params(("parallel", "parallel", "arbitrary"), block_bytes,
                                2 * tm * tn * 4),
        name=name,
    )(*args)


def _mm_wcast_kernel(*refs, relu2, has_res):
    a_ref, w_ref = refs[0], refs[1]
    res_ref = refs[2] if has_res else None
    o_ref, wb_ref = refs[2 + has_res], refs[3 + has_res]

    @pl.when(pl.program_id(1) == 0)
    def _():
        wb_ref[...] = w_ref[...].astype(BF16)

    acc = jnp.dot(a_ref[...], wb_ref[...], preferred_element_type=F32)
    if relu2:
        acc = jnp.square(jnp.maximum(acc, 0.0))
    if has_res:
        acc = acc + res_ref[...]
    o_ref[...] = acc.astype(o_ref.dtype)


def _matmul_wcast(a, w, *, tm, tn, out_dtype, relu2=False, res=None, name, w_cols=None):
    m, kdim = a.shape
    n, w_col = (w.shape[1], lambda j: j) if w_cols is None else w_cols
    tm = min(tm, m)
    in_specs = [pl.BlockSpec((tm, kdim), lambda j, i: (i, 0)),
                pl.BlockSpec((kdim, tn), lambda j, i: (0, w_col(j)))]
    args = [a, w]
    out_bytes = jnp.dtype(out_dtype).itemsize
    block_bytes = tm * kdim * 2 + kdim * tn * 4 + tm * tn * out_bytes
    if res is not None:
        in_specs.append(pl.BlockSpec((tm, tn), lambda j, i: (i, j)))
        args.append(res)
        block_bytes += tm * tn * 4
    return pl.pallas_call(
        functools.partial(_mm_wcast_kernel, relu2=relu2, has_res=res is not None),
        grid=(n // tn, m // tm),
        in_specs=in_specs,
        out_specs=pl.BlockSpec((tm, tn), lambda j, i: (i, j)),
        out_shape=jax.ShapeDtypeStruct((m, n), out_dtype),
        scratch_shapes=[pltpu.VMEM((kdim, tn), BF16)],
        compiler_params=_params(("arbitrary", "arbitrary"), block_bytes,
                                kdim * tn * 2 + 2 * tm * tn * 4),
        name=name,
    )(*args)


def _split_terms(x, n):
    terms = []
    for _ in range(n - 1):
        t = x.astype(BF16)
        terms.append(t)
        x = x - t.astype(F32)
    terms.append(x.astype(BF16))
    return terms


def _qk_prep_kernel(x_ref, cos_ref, sin_lo_ref, sin_hi_ref, g_ref, ones_ref, o_ref, *,
                    heads_per_block, transpose_out):
    g, cos, sin_lo, sin_hi = g_ref[...], cos_ref[...], sin_lo_ref[...], sin_hi_ref[...]
    for hh in range(heads_per_block):
        sl = slice(hh * HEAD_W, (hh + 1) * HEAD_W)
        x = x_ref[:, sl].astype(F32)
        sq = jnp.concatenate(_split_terms(x * x, 2), axis=1)
        ss = jnp.dot(sq, ones_ref[...], preferred_element_type=F32)
        r = lax.rsqrt(ss * (1.0 / QK_DIM) + EPS)
        y = (x * r) * g
        y = (y * cos + pltpu.roll(y, HEAD_W - ROPE_DIMS // 2, 1) * sin_lo
             + pltpu.roll(y, ROPE_DIMS // 2, 1) * sin_hi)
        if transpose_out:
            o_ref[0, hh, 0] = y.T.astype(o_ref.dtype)
        else:
            o_ref[:, sl] = y.astype(o_ref.dtype)


def _qk_prep(proj, col0, tables, gain, tm, rows_per_seq, heads_per_block, transpose_out):
    m = proj.shape[0]
    nblk = rows_per_seq // tm
    tw = heads_per_block * HEAD_W
    lane_group = jnp.arange(HEAD_W) // QK_DIM
    group_ones = jnp.tile((lane_group[:, None] == lane_group[None, :]).astype(BF16), (2, 1))
    tab_spec = pl.BlockSpec((tm, HEAD_W), lambda i, j: (i % nblk, 0))
    if transpose_out:
        out_spec = pl.BlockSpec((1, heads_per_block, 1, HEAD_W, tm),
                                lambda i, j: (i // nblk, j, i % nblk, 0, 0))
        out_shape = jax.ShapeDtypeStruct((m // rows_per_seq, HEADS, nblk, HEAD_W, tm), BF16)
    else:
        out_spec = pl.BlockSpec((tm, tw), lambda i, j: (i, j))
        out_shape = jax.ShapeDtypeStruct((m, HEADS * HEAD_W), BF16)
    return pl.pallas_call(
        functools.partial(_qk_prep_kernel, heads_per_block=heads_per_block,
                          transpose_out=transpose_out),
        grid=(m // tm, HEADS // heads_per_block),
        in_specs=[pl.BlockSpec((tm, tw), lambda i, j: (i, col0 * HEAD_W // tw + j)),
                  tab_spec, tab_spec, tab_spec,
                  pl.BlockSpec((1, HEAD_W), lambda i, j: (0, 0)),
                  pl.BlockSpec((2 * HEAD_W, HEAD_W), lambda i, j: (0, 0))],
        out_specs=out_spec,
        out_shape=out_shape,
        compiler_params=_params(("parallel", "parallel"), tm * tw * 4 + 3 * tm * HEAD_W * 4,
                                tm * HEAD_W * 48),
        name="k_prep" if transpose_out else "q_prep",
    )(proj, *tables, gain.reshape(1, HEAD_W), group_ones)


def _rope_tables(pos):
    half = ROPE_DIMS // 2
    inv_freq = ROPE_THETA ** (-(jnp.arange(half, dtype=F32) * 2.0) / ROPE_DIMS)
    ang = pos.astype(F32)[:, None] * inv_freq[None, :]
    cos, sin = jnp.cos(ang), jnp.sin(ang)
    n = pos.shape[0]
    pad = QK_DIM - ROPE_DIMS
    ones, zeros, zh = jnp.ones((n, pad), F32), jnp.zeros((n, pad), F32), jnp.zeros((n, half), F32)
    cos_t = jnp.concatenate([cos, cos, ones], axis=1)
    sin_lo = jnp.concatenate([-sin, zh, zeros], axis=1)
    sin_hi = jnp.concatenate([zh, sin, zeros], axis=1)
    return tuple(jnp.tile(t, (1, 2)) for t in (cos_t, sin_lo, sin_hi))


def _nt_dot(a, b):
    return lax.dot_general(a, b, (((1,), (1,)), ((), ())), preferred_element_type=F32)


def _attn_kernel(q_ref, kt_ref, v_ref, ktm_ref, vm_ref, lam_ref, subln_ref, o_ref,
                 vp_ref, vpm_ref, *, tq, nq):
    qi = pl.program_id(2)

    @pl.when(qi == 0)
    def _():
        ones = jnp.ones((vp_ref.shape[0], HEAD_W), BF16)
        vp_ref[:, :HEAD_W] = v_ref[...]
        vp_ref[:, HEAD_W:] = ones
        vpm_ref[:, :HEAD_W] = vm_ref[...]
        vpm_ref[:, HEAD_W:] = ones[:META_ROWS]

    def scores(q_maps, kt):
        return [jnp.dot(qc, kt, preferred_element_type=F32) for qc in q_maps]

    def consume(state, s_maps, vp, mask):
        out = []
        for (m_old, acc), s in zip(state, s_maps):
            if mask is not None:
                s = jnp.where(mask, s, NEG_INF)
            m_new = jnp.maximum(m_old, jnp.max(s, axis=-1, keepdims=True))
            p = jnp.exp2((s - pltpu.repeat(m_new, s.shape[1] // HEAD_W, axis=1)).astype(BF16))
            alpha = jnp.exp2(m_old - m_new)
            acc = (pltpu.repeat(alpha, 2, axis=1) * acc
                   + jnp.dot(p, vp, preferred_element_type=F32))
            out.append((m_new, acc))
        return out

    def attend(n_blocks):
        q = q_ref[...]
        lane = lax.broadcasted_iota(jnp.int32, q.shape, 1)
        zero = jnp.zeros_like(q)
        q_maps = (jnp.where(lane < QK_DIM, q, zero),
                  jnp.where(lane >= QK_DIM, q, zero))
        state = [(jnp.full((tq, HEAD_W), NEG_INF, F32), jnp.zeros((tq, 2 * HEAD_W), F32))] * 2
        col_m = lax.broadcasted_iota(jnp.int32, (tq, META_ROWS), 1)
        row = lax.broadcasted_iota(jnp.int32, (tq, tq), 0)
        col = lax.broadcasted_iota(jnp.int32, (tq, tq), 1)
        s_meta = scores(q_maps, ktm_ref[0, 0, 0])
        s_next = scores(q_maps, kt_ref[0, 0, 0])
        state = consume(state, s_meta, vpm_ref[...], col_m >= META_ROWS - N_META)
        for j in range(n_blocks):
            s_cur = s_next
            if j + 1 < n_blocks:
                s_next = scores(q_maps, kt_ref[0, 0, j + 1])
            state = consume(state, s_cur, vp_ref[j * tq:(j + 1) * tq, :],
                            col <= row if j + 1 == n_blocks else None)

        lp = lam_ref[...]
        lam = (jnp.exp(jnp.sum(lp[0:1] * lp[1:2], axis=-1, keepdims=True))
               - jnp.exp(jnp.sum(lp[2:3] * lp[3:4], axis=-1, keepdims=True)) + LAM_INIT)
        a1, a2 = state[0][1], state[1][1]
        o = (a1[:, :HEAD_W] / a1[:, HEAD_W:HEAD_W + 1]
             - lam * (a2[:, :HEAD_W] / a2[:, HEAD_W:HEAD_W + 1]))
        r = lax.rsqrt(jnp.mean(o * o, axis=-1, keepdims=True) + EPS)
        o_ref[...] = (((o * r) * subln_ref[...]) * (1.0 - LAM_INIT)).astype(o_ref.dtype)

    for qv in range(nq):
        pl.when(qi == qv)(functools.partial(attend, qv + 1))


def _attention(q, kt, proj, kt_meta, proj_meta, lam_params, subln, batch, seq, tq):
    nq = seq // tq
    kernel = functools.partial(_attn_kernel, tq=tq, nq=nq)
    blk = (2 * tq * HEAD_W * 2 + 2 * seq * HEAD_W * 2 + 2 * META_ROWS * HEAD_W * 2)
    scratch_bytes = (seq + META_ROWS) * 2 * HEAD_W * 2
    return pl.pallas_call(
        kernel,
        grid=(batch, HEADS, nq),
        in_specs=[pl.BlockSpec((tq, HEAD_W), lambda b, h, i: (b * nq + i, h)),
                  pl.BlockSpec((1, 1, nq, HEAD_W, tq), lambda b, h, i: (b, h, 0, 0, 0)),
                  pl.BlockSpec((seq, HEAD_W), lambda b, h, i: (b, COL_DA_V + h)),
                  pl.BlockSpec((1, 1, 1, HEAD_W, META_ROWS), lambda b, h, i: (0, h, 0, 0, 0)),
                  pl.BlockSpec((META_ROWS, HEAD_W), lambda b, h, i: (0, META_COL_V + h)),
                  pl.BlockSpec((4, QK_DIM), lambda b, h, i: (0, 0)),
                  pl.BlockSpec((1, HEAD_W), lambda b, h, i: (0, 0))],
        out_specs=pl.BlockSpec((tq, HEAD_W), lambda b, h, i: (b * nq + i, h)),
        out_shape=jax.ShapeDtypeStruct((batch * seq, HEADS * HEAD_W), BF16),
        scratch_shapes=[pltpu.VMEM((seq, 2 * HEAD_W), BF16),
                        pltpu.VMEM((META_ROWS, 2 * HEAD_W), BF16)],
        compiler_params=_params(("parallel", "parallel", "arbitrary"), blk,
                                scratch_bytes + 8 * tq * tq * 4),
        name="diff_attention",
    )(q, kt, proj, kt_meta, proj_meta, lam_params, subln)


def _lower_bound(lb_ref):
    x = lb_ref[...]
    e = jnp.exp(x - jnp.max(x, axis=0, keepdims=True))
    return e[0:1] / jnp.sum(e, axis=0, keepdims=True)


def _chunk_tri(rows):
    r = jnp.arange(rows)
    same_chunk = r[:, None] // HG_CHUNK == r[None, :] // HG_CHUNK
    return (same_chunk & (r[None, :] <= r[:, None])).astype(BF16)


def _chunk_cumsum(tri, g):
    parts = jnp.concatenate(_split_terms(g, 3), axis=1)
    s = jnp.dot(tri, parts, preferred_element_type=F32)
    return s[:, :HEAD_W] + s[:, HEAD_W:2 * HEAD_W] + s[:, 2 * HEAD_W:]


def _hg_block(q, fl, v, lb, tri, st, valid=None):
    c = HG_CHUNK
    n = q.shape[0] // c
    f = lb + (1.0 - lb) * jax.nn.sigmoid(fl)
    g = jnp.log(f)
    kk = 1.0 - f
    if valid is not None:
        g = jnp.where(valid, g, 0.0)
        kk = jnp.where(valid, kk, 0.0)
        v = jnp.where(valid, v, jnp.zeros_like(v))
    b = _chunk_cumsum(tri, g).reshape(n, c, HEAD_W)
    q3, k3, v3 = q.reshape(n, c, HEAD_W), kk.reshape(n, c, HEAD_W), v.reshape(n, c, HEAD_W)
    b_mid = b[:, c // 2 - 1:c // 2, :]
    b_last = b[:, c - 1:c, :]
    q_in = (q3 * jnp.exp(b - b_mid)).astype(BF16)
    k_in = (k3 * jnp.exp(b_mid - b)).astype(BF16)
    k_dec = (k3 * jnp.exp(b_last - b)).astype(BF16)
    q_ex = (q3 * jnp.exp(b)).astype(BF16)
    decay = jnp.exp(b_last)
    row = lax.broadcasted_iota(jnp.int32, (n, c, c), 1)
    col = lax.broadcasted_iota(jnp.int32, (n, c, c), 2)
    a = lax.dot_general(q_in, k_in, (((2,), (2,)), ((0,), (0,))), preferred_element_type=F32)
    a = jnp.where(col <= row, a, 0.0).astype(BF16)
    o_intra = lax.dot_general(a, v3, (((2,), (1,)), ((0,), (0,))), preferred_element_type=F32)
    u_t = [lax.dot_general(v3[i], k_dec[i], (((0,), (0,)), ((), ())), preferred_element_type=F32)
           for i in range(n)]
    outs = []
    for i in range(n):
        outs.append(o_intra[i] + _nt_dot(q_ex[i], st.astype(BF16)))
        st = st * decay[i] + u_t[i]
    return outs, st


def _hg_meta_kernel(f_ref, i_ref, tri_ref, lb_ref, s_ref):
    row = lax.broadcasted_iota(jnp.int32, (HG_CHUNK, HEAD_W), 0)
    _, st = _hg_block(jnp.zeros((HG_CHUNK, HEAD_W), F32), f_ref[...].astype(F32), i_ref[...],
                      _lower_bound(lb_ref), tri_ref[...], jnp.zeros((HEAD_W, HEAD_W), F32),
                      valid=row >= HG_CHUNK - N_META)
    s_ref[0] = st


def _hg_meta_state(proj_meta, lower_bound):
    blk = META_ROWS // HG_CHUNK - 1
    return pl.pallas_call(
        _hg_meta_kernel,
        grid=(HEADS,),
        in_specs=[pl.BlockSpec((HG_CHUNK, HEAD_W), lambda h: (blk, META_COL_F + h)),
                  pl.BlockSpec((HG_CHUNK, HEAD_W), lambda h: (blk, META_COL_I + h)),
                  pl.BlockSpec((HG_CHUNK, HG_CHUNK), lambda h: (0, 0)),
                  pl.BlockSpec((lower_bound.shape[0], HEAD_W), lambda h: (0, h))],
        out_specs=pl.BlockSpec((1, HEAD_W, HEAD_W), lambda h: (h, 0, 0)),
        out_shape=jax.ShapeDtypeStruct((HEADS, HEAD_W, HEAD_W), F32),
        compiler_params=_params(("parallel",), 1 << 20),
        name="hgrn2_meta_state",
    )(proj_meta, proj_meta, _chunk_tri(HG_CHUNK), lower_bound)


def _hgrn_kernel(q_ref, f_ref, i_ref, g_ref, tri_ref, lb_ref, s0_ref, on_ref, o_ref, st_ref):
    @pl.when(pl.program_id(2) == 0)
    def _():
        st_ref[...] = s0_ref[0]

    outs, st = _hg_block(q_ref[...].astype(F32), f_ref[...].astype(F32), i_ref[...],
                         _lower_bound(lb_ref), tri_ref[...], st_ref[...])
    st_ref[...] = st
    for i, o in enumerate(outs):
        sl = pl.ds(i * HG_CHUNK, HG_CHUNK)
        r = lax.rsqrt(jnp.mean(o * o, axis=-1, keepdims=True) + EPS)
        y = ((o * r) * on_ref[...]) * jax.nn.sigmoid(g_ref[sl, :].astype(F32))
        o_ref[sl, :] = y.astype(o_ref.dtype)


def _hgrn2(proj, s0, lower_bound, out_norm, batch, seq, tb):
    nt = seq // tb

    def col(off):
        return pl.BlockSpec((tb, HEAD_W), lambda b, h, t: (b * nt + t, off + h))

    return pl.pallas_call(
        _hgrn_kernel,
        grid=(batch, HEADS, nt),
        in_specs=[col(COL_HG_Q), col(COL_HG_F), col(COL_HG_I), col(COL_HG_G),
                  pl.BlockSpec((tb, tb), lambda b, h, t: (0, 0)),
                  pl.BlockSpec((lower_bound.shape[0], HEAD_W), lambda b, h, t: (0, h)),
                  pl.BlockSpec((1, HEAD_W, HEAD_W), lambda b, h, t: (h, 0, 0)),
                  pl.BlockSpec((1, HEAD_W), lambda b, h, t: (0, 0))],
        out_specs=pl.BlockSpec((tb, HEAD_W), lambda b, h, t: (b * nt + t, h)),
        out_shape=jax.ShapeDtypeStruct((batch * seq, HEADS * HEAD_W), BF16),
        scratch_shapes=[pltpu.VMEM((HEAD_W, HEAD_W), F32)],
        compiler_params=_params(("parallel", "parallel", "arbitrary"),
                                5 * tb * HEAD_W * 2 + tb * tb * 2, 24 * tb * HEAD_W * 4),
        name="hgrn2",
    )(proj, proj, proj, proj, _chunk_tri(tb), lower_bound, s0, out_norm)


def _merge_kernel(ya_ref, yb_ref, wa_ref, wb_ref, ga_ref, gb_ref, o_ref, wa_bf_ref, wb_bf_ref):
    @pl.when(pl.program_id(1) == 0)
    def _():
        wa_bf_ref[...] = wa_ref[...].astype(BF16)
        wb_bf_ref[...] = wb_ref[...].astype(BF16)

    a = jnp.dot(ya_ref[...], wa_bf_ref[...], preferred_element_type=F32)
    b = jnp.dot(yb_ref[...], wb_bf_ref[...], preferred_element_type=F32)
    o = (jax.nn.sigmoid(ga_ref[...].astype(F32)) * a + jax.nn.sigmoid(gb_ref[...].astype(F32)) * b)
    o_ref[...] = o.astype(o_ref.dtype)


def _gated_merge(y_a, y_b, w_a, w_b, proj, tm, tn):
    m, kdim = y_a.shape
    n = w_a.shape[1]
    tm = min(tm, m)
    blk = 2 * tm * kdim * 2 + 2 * kdim * tn * 4 + 3 * tm * tn * 2
    return pl.pallas_call(
        _merge_kernel,
        grid=(n // tn, m // tm),
        in_specs=[pl.BlockSpec((tm, kdim), lambda j, i: (i, 0)),
                  pl.BlockSpec((tm, kdim), lambda j, i: (i, 0)),
                  pl.BlockSpec((kdim, tn), lambda j, i: (0, j)),
                  pl.BlockSpec((kdim, tn), lambda j, i: (0, j)),
                  pl.BlockSpec((tm, tn), lambda j, i: (i, COL_GATE_A // tn + j)),
                  pl.BlockSpec((tm, tn), lambda j, i: (i, COL_GATE_B // tn + j))],
        out_specs=pl.BlockSpec((tm, tn), lambda j, i: (i, j)),
        out_shape=jax.ShapeDtypeStruct((m, n), BF16),
        scratch_shapes=[pltpu.VMEM((kdim, tn), BF16), pltpu.VMEM((kdim, tn), BF16)],
        compiler_params=_params(("arbitrary", "arbitrary"), blk,
                                2 * kdim * tn * 2 + 4 * tm * tn * 4),
        name="gated_merge",
    )(y_a, y_b, w_a, w_b, proj, proj)


def kernel(x, meta_tokens, norm_mix, w_in, da_q_norm, da_k_norm, da_lambda_q1, da_lambda_k1,
           da_lambda_q2, da_lambda_k2, da_subln, hg_lower_bound, hg_out_norm, w_up_a, w_up_b,
           w_out, norm_mlp, w_ff1, w_ff2):
    batch, seq, d = x.shape
    assert d == D_MODEL and w_in.shape[0] == 1 and w_in.shape[2] == IN_WIDTH
    assert meta_tokens.shape == (N_META, D_MODEL) and seq % 512 == 0
    m = batch * seq
    xf = x.reshape(m, d)
    meta_pad = jnp.concatenate([jnp.zeros((META_ROWS - N_META, d), F32), meta_tokens.astype(F32)])

    tn = 512
    u = _rmsnorm(xf, norm_mix[0], 256)
    u_meta = _rmsnorm(meta_pad, norm_mix[0], META_ROWS)
    proj = _matmul_wcast(u, w_in[0], tm=1024, tn=tn, out_dtype=BF16, name="in_proj")
    kv_blocks = 2 * HEADS * HEAD_W // tn

    def meta_col(j):
        return jnp.where(j < kv_blocks, COL_DA_K * HEAD_W // tn + j,
                         COL_HG_F * HEAD_W // tn + j - kv_blocks)

    proj_meta = _matmul_wcast(u_meta, w_in[0], tm=META_ROWS, tn=tn, out_dtype=BF16,
                              name="in_proj_meta", w_cols=(META_WIDTH, meta_col))

    q_gain = jnp.tile(da_q_norm[0], 2) * (QK_DIM ** -0.5 * math.log2(math.e))
    k_gain = jnp.tile(da_k_norm[0], 2)
    pos_meta = jnp.maximum(jnp.arange(META_ROWS, dtype=jnp.int32) - (META_ROWS - N_META), 0)
    tab_main = _rope_tables(jnp.arange(seq, dtype=jnp.int32) + N_META)
    q_hat = _qk_prep(proj, COL_DA_Q, tab_main, q_gain, ATT_BLOCK, seq, 4, False)
    kt = _qk_prep(proj, COL_DA_K, tab_main, k_gain, ATT_BLOCK, seq, 4, True)
    kt_meta = _qk_prep(proj_meta, META_COL_K, _rope_tables(pos_meta), k_gain, META_ROWS, META_ROWS,
                       4, True)
    lam_params = jnp.stack([da_lambda_q1[0], da_lambda_k1[0], da_lambda_q2[0], da_lambda_k2[0]])
    y_a = _attention(q_hat, kt, proj, kt_meta, proj_meta, lam_params,
                     da_subln[0].reshape(1, HEAD_W), batch, seq, ATT_BLOCK)

    s0 = _hg_meta_state(proj_meta, hg_lower_bound)
    y_b = _hgrn2(proj, s0, hg_lower_bound, hg_out_norm[0].reshape(1, HEAD_W), batch, seq, 512)

    merged = _gated_merge(y_a, y_b, w_up_a[0], w_up_b[0], proj, 1024, tn)
    h1 = _matmul_wcast(merged, w_out[0], tm=1024, tn=tn, out_dtype=F32, res=xf, name="out_proj")

    v = _rmsnorm(h1, norm_mlp[0], 256)
    hid = _matmul_wcast(v, w_ff1[0], tm=1024, tn=tn, out_dtype=BF16, relu2=True, name="ff1")
    out = _matmul(hid, w_ff2[0].astype(BF16), tm=1024, tn=1024, tk=2048, out_dtype=F32, res=h1,
                  name="ff2")
    return out.reshape(batch, seq, d)
```

```python
import functools
import math

import jax
import jax.numpy as jnp
from jax import lax
from jax.experimental import pallas as pl
from jax.experimental.pallas import tpu as pltpu

F32 = jnp.float32
BF16 = jnp.bfloat16

D_MODEL = 4096
N_META = 16
HEADS = 16
HEAD_W = 128
QK_DIM = 64
ROPE_DIMS = 16
ROPE_THETA = 500000.0
HG_CHUNK = 64
META_ROWS = 128
ATT_BLOCK = 512
EPS = 1e-6
NEG_INF = -1e30
LAM_INIT = 0.8 - 0.6 * math.exp(-0.3 * 0)

COL_DA_Q, COL_DA_K, COL_DA_V = 0, 16, 32
COL_HG_Q, COL_HG_F, COL_HG_I, COL_HG_G = 48, 64, 80, 96
COL_GATE_A, COL_GATE_B = 112 * HEAD_W, 144 * HEAD_W
IN_WIDTH = 176 * HEAD_W
META_COL_K, META_COL_V, META_COL_F, META_COL_I = 0, 16, 32, 48
META_WIDTH = 64 * HEAD_W

V7X_VMEM_LIMIT_CAP = 58 * 2**20


def _params(semantics, block_bytes, extra_bytes=0):
    need = 2 * block_bytes + extra_bytes + (4 << 20)
    return pltpu.CompilerParams(dimension_semantics=semantics,
                                vmem_limit_bytes=min(max(need, 16 << 20), V7X_VMEM_LIMIT_CAP))


def _rmsnorm_kernel(x_ref, g_ref, o_ref):
    x = x_ref[...]
    r = lax.rsqrt(jnp.mean(x * x, axis=-1, keepdims=True) + EPS)
    o_ref[...] = ((x * r) * g_ref[...]).astype(o_ref.dtype)


def _rmsnorm(x, g, tm):
    m, d = x.shape
    return pl.pallas_call(
        _rmsnorm_kernel,
        grid=(m // tm,),
        in_specs=[pl.BlockSpec((tm, d), lambda i: (i, 0)),
                  pl.BlockSpec((1, d), lambda i: (0, 0))],
        out_specs=pl.BlockSpec((tm, d), lambda i: (i, 0)),
        out_shape=jax.ShapeDtypeStruct((m, d), BF16),
        compiler_params=_params(("parallel",), tm * d * 6),
        name="rmsnorm",
    )(x, g.reshape(1, d))


def _mm_kernel(*refs, nk, relu2, has_res):
    a_ref, w_ref = refs[0], refs[1]
    res_ref = refs[2] if has_res else None
    o_ref = refs[2 + has_res]
    part = jnp.dot(a_ref[...], w_ref[...], preferred_element_type=F32)

    def finish(acc):
        if relu2:
            acc = jnp.square(jnp.maximum(acc, 0.0))
        if has_res:
            acc = acc + res_ref[...]
        o_ref[...] = acc.astype(o_ref.dtype)

    if nk == 1:
        finish(part)
        return
    assert has_res and not relu2 and o_ref.dtype == F32
    k = pl.program_id(2)

    @pl.when(k == 0)
    def _():
        o_ref[...] = part + res_ref[...]

    @pl.when(k > 0)
    def _():
        o_ref[...] += part


def _matmul(a, w, *, tm, tn, tk, out_dtype, relu2=False, res=None, name):
    m, kdim = a.shape
    n = w.shape[1]
    tm = min(tm, m)
    nk = kdim // tk
    in_specs = [pl.BlockSpec((tm, tk), lambda i, j, k: (i, k)),
                pl.BlockSpec((tk, tn), lambda i, j, k: (k, j))]
    args = [a, w]
    out_bytes = jnp.dtype(out_dtype).itemsize
    block_bytes = tm * tk * 2 + tk * tn * 2 + tm * tn * out_bytes
    if res is not None:
        in_specs.append(pl.BlockSpec((tm, tn), lambda i, j, k: (i, j)))
        args.append(res)
        block_bytes += tm * tn * 4
    return pl.pallas_call(
        functools.partial(_mm_kernel, nk=nk, relu2=relu2, has_res=res is not None),
        grid=(m // tm, n // tn, nk),
        in_specs=in_specs,
        out_specs=pl.BlockSpec((tm, tn), lambda i, j, k: (i, j)),
        out_shape=jax.ShapeDtypeStruct((m, n), out_dtype),
        compiler_params=_params(("parallel", "parallel", "arbitrary"), block_bytes,
                                2 * tm * tn * 4),
        name=name,
    )(*args)


def _mm_wcast_kernel(*refs, relu2, has_res):
    a_ref, w_ref = refs[0], refs[1]
    res_ref = refs[2] if has_res else None
    o_ref, wb_ref = refs[2 + has_res], refs[3 + has_res]

    @pl.when(pl.program_id(1) == 0)
    def _():
        wb_ref[...] = w_ref[...].astype(BF16)

    acc = jnp.dot(a_ref[...], wb_ref[...], preferred_element_type=F32)
    if relu2:
        acc = jnp.square(jnp.maximum(acc, 0.0))
    if has_res:
        acc = acc + res_ref[...]
    o_ref[...] = acc.astype(o_ref.dtype)


def _matmul_wcast(a, w, *, tm, tn, out_dtype, relu2=False, res=None, name, w_cols=None):
    m, kdim = a.shape
    n, w_col = (w.shape[1], lambda j: j) if w_cols is None else w_cols
    tm = min(tm, m)
    in_specs = [pl.BlockSpec((tm, kdim), lambda j, i: (i, 0)),
                pl.BlockSpec((kdim, tn), lambda j, i: (0, w_col(j)))]
    args = [a, w]
    out_bytes = jnp.dtype(out_dtype).itemsize
    block_bytes = tm * kdim * 2 + kdim * tn * 4 + tm * tn * out_bytes
    if res is not None:
        in_specs.append(pl.BlockSpec((tm, tn), lambda j, i: (i, j)))
        args.append(res)
        block_bytes += tm * tn * 4
    return pl.pallas_call(
        functools.partial(_mm_wcast_kernel, relu2=relu2, has_res=res is not None),
        grid=(n // tn, m // tm),
        in_specs=in_specs,
        out_specs=pl.BlockSpec((tm, tn), lambda j, i: (i, j)),
        out_shape=jax.ShapeDtypeStruct((m, n), out_dtype),
        scratch_shapes=[pltpu.VMEM((kdim, tn), BF16)],
        compiler_params=_params(("arbitrary", "arbitrary"), block_bytes,
                                kdim * tn * 2 + 2 * tm * tn * 4),
        name=name,
    )(*args)


def _mm_stream_kernel(*refs, relu2, has_res, n_col, n_row):
    a_ref, w_hbm = refs[0], refs[1]
    res_ref = refs[2] if has_res else None
    o_ref, wb_ref, stage_ref, sem = refs[2 + has_res:6 + has_res]
    j, i = pl.program_id(0), pl.program_id(1)
    ck, tn = stage_ref.shape[1], stage_ref.shape[2]

    def chunk_copy(col, c, slot):
        return pltpu.make_async_copy(
            w_hbm.at[pl.ds(pl.multiple_of(c * ck, ck), ck), pl.ds(pl.multiple_of(col * tn, tn), tn)],
            stage_ref.at[slot], sem.at[slot])

    def lookahead_col(jj):
        return jnp.minimum(jj + 1, n_col - 1)

    @pl.when((j == 0) & (i == 0))
    def _():
        for c in range(n_row):
            cp = chunk_copy(0, c, c % 2)
            cp.start()
            cp.wait()
            wb_ref[0, c * ck:(c + 1) * ck, :] = stage_ref[c % 2].astype(BF16)
        chunk_copy(lookahead_col(0), 0, 0).start()

    acc = jnp.dot(a_ref[...], wb_ref[j % 2], preferred_element_type=F32)
    if relu2:
        acc = jnp.square(jnp.maximum(acc, 0.0))
    if has_res:
        acc = acc + res_ref[...]
    o_ref[...] = acc.astype(o_ref.dtype)

    chunk_copy(lookahead_col(j), i, i % 2).wait()
    wb_ref[(j + 1) % 2, pl.ds(pl.multiple_of(i * ck, ck), ck), :] = stage_ref[i % 2].astype(BF16)

    @pl.when((j < n_col - 1) | (i < n_row - 1))
    def _():
        wrap = i == n_row - 1
        chunk_copy(lookahead_col(jnp.where(wrap, j + 1, j)), jnp.where(wrap, 0, i + 1),
                   (i + 1) % 2).start()


def _matmul_stream(a, w, *, tm, tn, out_dtype, relu2=False, res=None, name):
    m, kdim = a.shape
    n = w.shape[1]
    n_row, n_col = m // tm, n // tn
    assert n_row % 2 == 0 and n_col >= 2 and kdim % n_row == 0
    ck = kdim // n_row
    in_specs = [pl.BlockSpec((tm, kdim), lambda j, i: (i, 0)),
                pl.BlockSpec(memory_space=pl.ANY)]
    args = [a, w]
    out_bytes = jnp.dtype(out_dtype).itemsize
    block_bytes = tm * kdim * 2 + tm * tn * out_bytes
    if res is not None:
        in_specs.append(pl.BlockSpec((tm, tn), lambda j, i: (i, j)))
        args.append(res)
        block_bytes += tm * tn * 4
    return pl.pallas_call(
        functools.partial(_mm_stream_kernel, relu2=relu2, has_res=res is not None,
                          n_col=n_col, n_row=n_row),
        grid=(n_col, n_row),
        in_specs=in_specs,
        out_specs=pl.BlockSpec((tm, tn), lambda j, i: (i, j)),
        out_shape=jax.ShapeDtypeStruct((m, n), out_dtype),
        scratch_shapes=[pltpu.VMEM((2, kdim, tn), BF16), pltpu.VMEM((2, ck, tn), F32),
                        pltpu.SemaphoreType.DMA((2,))],
        compiler_params=_params(("arbitrary", "arbitrary"), block_bytes,
                                2 * kdim * tn * 2 + 2 * ck * tn * 4 + tm * tn * 4),
        name=name,
    )(*args)


def _split_terms(x, n):
    terms = []
    for _ in range(n - 1):
        t = x.astype(BF16)
        terms.append(t)
        x = x - t.astype(F32)
    terms.append(x.astype(BF16))
    return terms


def _qk_prep_kernel(x_ref, cos_ref, sin_lo_ref, sin_hi_ref, g_ref, ones_ref, o_ref, *,
                    heads_per_block, transpose_out):
    g, cos, sin_lo, sin_hi = g_ref[...], cos_ref[...], sin_lo_ref[...], sin_hi_ref[...]
    for hh in range(heads_per_block):
        sl = slice(hh * HEAD_W, (hh + 1) * HEAD_W)
        x = x_ref[:, sl].astype(F32)
        sq = jnp.concatenate(_split_terms(x * x, 2), axis=1)
        ss = jnp.dot(sq, ones_ref[...], preferred_element_type=F32)
        r = lax.rsqrt(ss * (1.0 / QK_DIM) + EPS)
        y = (x * r) * g
        y = (y * cos + pltpu.roll(y, HEAD_W - ROPE_DIMS // 2, 1) * sin_lo
             + pltpu.roll(y, ROPE_DIMS // 2, 1) * sin_hi)
        if transpose_out:
            o_ref[0, hh, 0] = y.T.astype(o_ref.dtype)
        else:
            o_ref[:, sl] = y.astype(o_ref.dtype)


def _qk_prep(proj, col0, tables, gain, tm, rows_per_seq, heads_per_block, transpose_out):
    m = proj.shape[0]
    nblk = rows_per_seq // tm
    tw = heads_per_block * HEAD_W
    lane_group = jnp.arange(HEAD_W) // QK_DIM
    group_ones = jnp.tile((lane_group[:, None] == lane_group[None, :]).astype(BF16), (2, 1))
    tab_spec = pl.BlockSpec((tm, HEAD_W), lambda i, j: (i % nblk, 0))
    if transpose_out:
        out_spec = pl.BlockSpec((1, heads_per_block, 1, HEAD_W, tm),
                                lambda i, j: (i // nblk, j, i % nblk, 0, 0))
        out_shape = jax.ShapeDtypeStruct((m // rows_per_seq, HEADS, nblk, HEAD_W, tm), BF16)
    else:
        out_spec = pl.BlockSpec((tm, tw), lambda i, j: (i, j))
        out_shape = jax.ShapeDtypeStruct((m, HEADS * HEAD_W), BF16)
    return pl.pallas_call(
        functools.partial(_qk_prep_kernel, heads_per_block=heads_per_block,
                          transpose_out=transpose_out),
        grid=(m // tm, HEADS // heads_per_block),
        in_specs=[pl.BlockSpec((tm, tw), lambda i, j: (i, col0 * HEAD_W // tw + j)),
                  tab_spec, tab_spec, tab_spec,
                  pl.BlockSpec((1, HEAD_W), lambda i, j: (0, 0)),
                  pl.BlockSpec((2 * HEAD_W, HEAD_W), lambda i, j: (0, 0))],
        out_specs=out_spec,
        out_shape=out_shape,
        compiler_params=_params(("parallel", "parallel"), tm * tw * 4 + 3 * tm * HEAD_W * 4,
                                tm * HEAD_W * 48),
        name="k_prep" if transpose_out else "q_prep",
    )(proj, *tables, gain.reshape(1, HEAD_W), group_ones)


def _rope_tables(pos):
    half = ROPE_DIMS // 2
    inv_freq = ROPE_THETA ** (-(jnp.arange(half, dtype=F32) * 2.0) / ROPE_DIMS)
    ang = pos.astype(F32)[:, None] * inv_freq[None, :]
    cos, sin = jnp.cos(ang), jnp.sin(ang)
    n = pos.shape[0]
    pad = QK_DIM - ROPE_DIMS
    ones, zeros, zh = jnp.ones((n, pad), F32), jnp.zeros((n, pad), F32), jnp.zeros((n, half), F32)
    cos_t = jnp.concatenate([cos, cos, ones], axis=1)
    sin_lo = jnp.concatenate([-sin, zh, zeros], axis=1)
    sin_hi = jnp.concatenate([zh, sin, zeros], axis=1)
    return tuple(jnp.tile(t, (1, 2)) for t in (cos_t, sin_lo, sin_hi))


def _nt_dot(a, b):
    return lax.dot_general(a, b, (((1,), (1,)), ((), ())), preferred_element_type=F32)


def _lane_repeat(x, n):
    return jnp.concatenate([x] * n, axis=1)


def _attn_kernel(q_ref, kt_ref, v_ref, ktm_ref, vm_ref, lam_ref, subln_ref, o_ref,
                 vp_ref, vpm_ref, *, tq, nq):
    qi = pl.program_id(2)

    @pl.when(qi == 0)
    def _():
        ones = jnp.ones((vp_ref.shape[0], HEAD_W), BF16)
        vp_ref[:, :HEAD_W] = v_ref[...]
        vp_ref[:, HEAD_W:] = ones
        vpm_ref[:, :HEAD_W] = vm_ref[...]
        vpm_ref[:, HEAD_W:] = ones[:META_ROWS]

    def scores(q_maps, kt):
        return [jnp.dot(qc, kt, preferred_element_type=F32) for qc in q_maps]

    def consume(state, s_maps, vp, mask):
        out = []
        for (m_old, acc), s in zip(state, s_maps):
            if mask is not None:
                s = jnp.where(mask, s, NEG_INF)
            m_new = jnp.maximum(m_old, jnp.max(s, axis=-1, keepdims=True))
            p = jnp.exp2((s - _lane_repeat(m_new, s.shape[1] // HEAD_W)).astype(BF16))
            alpha = jnp.exp2(m_old - m_new)
            acc = (_lane_repeat(alpha, 2) * acc
                   + jnp.dot(p, vp, preferred_element_type=F32))
            out.append((m_new, acc))
        return out

    def attend(n_blocks):
        q = q_ref[...]
        lane = lax.broadcasted_iota(jnp.int32, q.shape, 1)
        zero = jnp.zeros_like(q)
        q_maps = (jnp.where(lane < QK_DIM, q, zero),
                  jnp.where(lane >= QK_DIM, q, zero))
        state = [(jnp.full((tq, HEAD_W), NEG_INF, F32), jnp.zeros((tq, 2 * HEAD_W), F32))] * 2
        col_m = lax.broadcasted_iota(jnp.int32, (tq, META_ROWS), 1)
        row = lax.broadcasted_iota(jnp.int32, (tq, tq), 0)
        col = lax.broadcasted_iota(jnp.int32, (tq, tq), 1)
        s_meta = scores(q_maps, ktm_ref[0, 0, 0])
        s_next = scores(q_maps, kt_ref[0, 0, 0])
        state = consume(state, s_meta, vpm_ref[...], col_m >= META_ROWS - N_META)
        for j in range(n_blocks):
            s_cur = s_next
            if j + 1 < n_blocks:
                s_next = scores(q_maps, kt_ref[0, 0, j + 1])
            state = consume(state, s_cur, vp_ref[j * tq:(j + 1) * tq, :],
                            col <= row if j + 1 == n_blocks else None)

        lp = lam_ref[...]
        lam = (jnp.exp(jnp.sum(lp[0:1] * lp[1:2], axis=-1, keepdims=True))
               - jnp.exp(jnp.sum(lp[2:3] * lp[3:4], axis=-1, keepdims=True)) + LAM_INIT)
        a1, a2 = state[0][1], state[1][1]
        o = (a1[:, :HEAD_W] / a1[:, HEAD_W:HEAD_W + 1]
             - lam * (a2[:, :HEAD_W] / a2[:, HEAD_W:HEAD_W + 1]))
        r = lax.rsqrt(jnp.mean(o * o, axis=-1, keepdims=True) + EPS)
        o_ref[...] = (((o * r) * subln_ref[...]) * (1.0 - LAM_INIT)).astype(o_ref.dtype)

    for qv in range(nq):
        pl.when(qi == qv)(functools.partial(attend, qv + 1))


def _attention(q, kt, proj, kt_meta, proj_meta, lam_params, subln, batch, seq, tq):
    nq = seq // tq
    kernel = functools.partial(_attn_kernel, tq=tq, nq=nq)
    blk = (2 * tq * HEAD_W * 2 + 2 * seq * HEAD_W * 2 + 2 * META_ROWS * HEAD_W * 2)
    scratch_bytes = (seq + META_ROWS) * 2 * HEAD_W * 2
    return pl.pallas_call(
        kernel,
        grid=(batch, HEADS, nq),
        in_specs=[pl.BlockSpec((tq, HEAD_W), lambda b, h, i: (b * nq + i, h)),
                  pl.BlockSpec((1, 1, nq, HEAD_W, tq), lambda b, h, i: (b, h, 0, 0, 0)),
                  pl.BlockSpec((seq, HEAD_W), lambda b, h, i: (b, COL_DA_V + h)),
                  pl.BlockSpec((1, 1, 1, HEAD_W, META_ROWS), lambda b, h, i: (0, h, 0, 0, 0)),
                  pl.BlockSpec((META_ROWS, HEAD_W), lambda b, h, i: (0, META_COL_V + h)),
                  pl.BlockSpec((4, QK_DIM), lambda b, h, i: (0, 0)),
                  pl.BlockSpec((1, HEAD_W), lambda b, h, i: (0, 0))],
        out_specs=pl.BlockSpec((tq, HEAD_W), lambda b, h, i: (b * nq + i, h)),
        out_shape=jax.ShapeDtypeStruct((batch * seq, HEADS * HEAD_W), BF16),
        scratch_shapes=[pltpu.VMEM((seq, 2 * HEAD_W), BF16),
                        pltpu.VMEM((META_ROWS, 2 * HEAD_W), BF16)],
        compiler_params=_params(("parallel", "parallel", "arbitrary"), blk,
                                scratch_bytes + 8 * tq * tq * 4),
        name="diff_attention",
    )(q, kt, proj, kt_meta, proj_meta, lam_params, subln)


def _lower_bound(lb_ref):
    x = lb_ref[...]
    e = jnp.exp(x - jnp.max(x, axis=0, keepdims=True))
    return e[0:1] / jnp.sum(e, axis=0, keepdims=True)


def _chunk_tri(rows):
    r = jnp.arange(rows)
    same_chunk = r[:, None] // HG_CHUNK == r[None, :] // HG_CHUNK
    return (same_chunk & (r[None, :] <= r[:, None])).astype(BF16)


def _chunk_cumsum(tri, g):
    parts = jnp.concatenate(_split_terms(g, 3), axis=1)
    s = jnp.dot(tri, parts, preferred_element_type=F32)
    return s[:, :HEAD_W] + s[:, HEAD_W:2 * HEAD_W] + s[:, 2 * HEAD_W:]


def _hg_block(q, fl, v, lb, tri, st, valid=None):
    c = HG_CHUNK
    n = q.shape[0] // c
    f = lb + (1.0 - lb) * jax.nn.sigmoid(fl)
    g = jnp.log(f)
    kk = 1.0 - f
    if valid is not None:
        g = jnp.where(valid, g, 0.0)
        kk = jnp.where(valid, kk, 0.0)
        v = jnp.where(valid, v, jnp.zeros_like(v))
    b = _chunk_cumsum(tri, g).reshape(n, c, HEAD_W)
    q3, k3, v3 = q.reshape(n, c, HEAD_W), kk.reshape(n, c, HEAD_W), v.reshape(n, c, HEAD_W)
    b_mid = b[:, c // 2 - 1:c // 2, :]
    b_last = b[:, c - 1:c, :]
    q_in = (q3 * jnp.exp(b - b_mid)).astype(BF16)
    k_in = (k3 * jnp.exp(b_mid - b)).astype(BF16)
    k_dec = (k3 * jnp.exp(b_last - b)).astype(BF16)
    q_ex = (q3 * jnp.exp(b)).astype(BF16)
    decay = jnp.exp(b_last)
    row = lax.broadcasted_iota(jnp.int32, (n, c, c), 1)
    col = lax.broadcasted_iota(jnp.int32, (n, c, c), 2)
    a = lax.dot_general(q_in, k_in, (((2,), (2,)), ((0,), (0,))), preferred_element_type=F32)
    a = jnp.where(col <= row, a, 0.0).astype(BF16)
    o_intra = lax.dot_general(a, v3, (((2,), (1,)), ((0,), (0,))), preferred_element_type=F32)
    u_t = [lax.dot_general(v3[i], k_dec[i], (((0,), (0,)), ((), ())), preferred_element_type=F32)
           for i in range(n)]
    outs = []
    for i in range(n):
        outs.append(o_intra[i] + _nt_dot(q_ex[i], st.astype(BF16)))
        st = st * decay[i] + u_t[i]
    return outs, st


def _hg_meta_kernel(f_ref, i_ref, tri_ref, lb_ref, s_ref):
    row = lax.broadcasted_iota(jnp.int32, (HG_CHUNK, HEAD_W), 0)
    _, st = _hg_block(jnp.zeros((HG_CHUNK, HEAD_W), F32), f_ref[...].astype(F32), i_ref[...],
                      _lower_bound(lb_ref), tri_ref[...], jnp.zeros((HEAD_W, HEAD_W), F32),
                      valid=row >= HG_CHUNK - N_META)
    s_ref[0] = st


def _hg_meta_state(proj_meta, lower_bound):
    blk = META_ROWS // HG_CHUNK - 1
    return pl.pallas_call(
        _hg_meta_kernel,
        grid=(HEADS,),
        in_specs=[pl.BlockSpec((HG_CHUNK, HEAD_W), lambda h: (blk, META_COL_F + h)),
                  pl.BlockSpec((HG_CHUNK, HEAD_W), lambda h: (blk, META_COL_I + h)),
                  pl.BlockSpec((HG_CHUNK, HG_CHUNK), lambda h: (0, 0)),
                  pl.BlockSpec((lower_bound.shape[0], HEAD_W), lambda h: (0, h))],
        out_specs=pl.BlockSpec((1, HEAD_W, HEAD_W), lambda h: (h, 0, 0)),
        out_shape=jax.ShapeDtypeStruct((HEADS, HEAD_W, HEAD_W), F32),
        compiler_params=_params(("parallel",), 1 << 20),
        name="hgrn2_meta_state",
    )(proj_meta, proj_meta, _chunk_tri(HG_CHUNK), lower_bound)


def _hgrn_kernel(q_ref, f_ref, i_ref, g_ref, tri_ref, lb_ref, s0_ref, on_ref, o_ref, st_ref):
    @pl.when(pl.program_id(2) == 0)
    def _():
        st_ref[...] = s0_ref[0]

    outs, st = _hg_block(q_ref[...].astype(F32), f_ref[...].astype(F32), i_ref[...],
                         _lower_bound(lb_ref), tri_ref[...], st_ref[...])
    st_ref[...] = st
    for i, o in enumerate(outs):
        sl = pl.ds(i * HG_CHUNK, HG_CHUNK)
        r = lax.rsqrt(jnp.mean(o * o, axis=-1, keepdims=True) + EPS)
        y = ((o * r) * on_ref[...]) * jax.nn.sigmoid(g_ref[sl, :].astype(F32))
        o_ref[sl, :] = y.astype(o_ref.dtype)


def _hgrn2(proj, s0, lower_bound, out_norm, batch, seq, tb):
    nt = seq // tb

    def col(off):
        return pl.BlockSpec((tb, HEAD_W), lambda b, h, t: (b * nt + t, off + h))

    return pl.pallas_call(
        _hgrn_kernel,
        grid=(batch, HEADS, nt),
        in_specs=[col(COL_HG_Q), col(COL_HG_F), col(COL_HG_I), col(COL_HG_G),
                  pl.BlockSpec((tb, tb), lambda b, h, t: (0, 0)),
                  pl.BlockSpec((lower_bound.shape[0], HEAD_W), lambda b, h, t: (0, h)),
                  pl.BlockSpec((1, HEAD_W, HEAD_W), lambda b, h, t: (h, 0, 0)),
                  pl.BlockSpec((1, HEAD_W), lambda b, h, t: (0, 0))],
        out_specs=pl.BlockSpec((tb, HEAD_W), lambda b, h, t: (b * nt + t, h)),
        out_shape=jax.ShapeDtypeStruct((batch * seq, HEADS * HEAD_W), BF16),
        scratch_shapes=[pltpu.VMEM((HEAD_W, HEAD_W), F32)],
        compiler_params=_params(("parallel", "parallel", "arbitrary"),
                                5 * tb * HEAD_W * 2 + tb * tb * 2, 24 * tb * HEAD_W * 4),
        name="hgrn2",
    )(proj, proj, proj, proj, _chunk_tri(tb), lower_bound, s0, out_norm)


def _merge_kernel(ya_ref, yb_ref, wa_ref, wb_ref, ga_ref, gb_ref, o_ref, wa_bf_ref, wb_bf_ref):
    @pl.when(pl.program_id(1) == 0)
    def _():
        wa_bf_ref[...] = wa_ref[...].astype(BF16)
        wb_bf_ref[...] = wb_ref[...].astype(BF16)

    a = jnp.dot(ya_ref[...], wa_bf_ref[...], preferred_element_type=F32)
    b = jnp.dot(yb_ref[...], wb_bf_ref[...], preferred_element_type=F32)
    o = (jax.nn.sigmoid(ga_ref[...].astype(F32)) * a + jax.nn.sigmoid(gb_ref[...].astype(F32)) * b)
    o_ref[...] = o.astype(o_ref.dtype)


def _gated_merge(y_a, y_b, w_a, w_b, proj, tm, tn):
    m, kdim = y_a.shape
    n = w_a.shape[1]
    tm = min(tm, m)
    blk = 2 * tm * kdim * 2 + 2 * kdim * tn * 4 + 3 * tm * tn * 2
    return pl.pallas_call(
        _merge_kernel,
        grid=(n // tn, m // tm),
        in_specs=[pl.BlockSpec((tm, kdim), lambda j, i: (i, 0)),
                  pl.BlockSpec((tm, kdim), lambda j, i: (i, 0)),
                  pl.BlockSpec((kdim, tn), lambda j, i: (0, j)),
                  pl.BlockSpec((kdim, tn), lambda j, i: (0, j)),
                  pl.BlockSpec((tm, tn), lambda j, i: (i, COL_GATE_A // tn + j)),
                  pl.BlockSpec((tm, tn), lambda j, i: (i, COL_GATE_B // tn + j))],
        out_specs=pl.BlockSpec((tm, tn), lambda j, i: (i, j)),
        out_shape=jax.ShapeDtypeStruct((m, n), BF16),
        scratch_shapes=[pltpu.VMEM((kdim, tn), BF16), pltpu.VMEM((kdim, tn), BF16)],
        compiler_params=_params(("arbitrary", "arbitrary"), blk,
                                2 * kdim * tn * 2 + 4 * tm * tn * 4),
        name="gated_merge",
    )(y_a, y_b, w_a, w_b, proj, proj)


def kernel(x, meta_tokens, norm_mix, w_in, da_q_norm, da_k_norm, da_lambda_q1, da_lambda_k1,
           da_lambda_q2, da_lambda_k2, da_subln, hg_lower_bound, hg_out_norm, w_up_a, w_up_b,
           w_out, norm_mlp, w_ff1, w_ff2):
    batch, seq, d = x.shape
    assert d == D_MODEL and w_in.shape[0] == 1 and w_in.shape[2] == IN_WIDTH
    assert meta_tokens.shape == (N_META, D_MODEL) and seq % 512 == 0
    m = batch * seq
    xf = x.reshape(m, d)
    meta_pad = jnp.concatenate([jnp.zeros((META_ROWS - N_META, d), F32), meta_tokens.astype(F32)])

    tn = 512
    u = _rmsnorm(xf, norm_mix[0], 256)
    u_meta = _rmsnorm(meta_pad, norm_mix[0], META_ROWS)
    proj = _matmul_stream(u, w_in[0], tm=1024, tn=1024, out_dtype=BF16, name="in_proj")
    kv_blocks = 2 * HEADS * HEAD_W // tn

    def meta_col(j):
        return jnp.where(j < kv_blocks, COL_DA_K * HEAD_W // tn + j,
                         COL_HG_F * HEAD_W // tn + j - kv_blocks)

    proj_meta = _matmul_wcast(u_meta, w_in[0], tm=META_ROWS, tn=tn, out_dtype=BF16,
                              name="in_proj_meta", w_cols=(META_WIDTH, meta_col))

    q_gain = jnp.tile(da_q_norm[0], 2) * (QK_DIM ** -0.5 * math.log2(math.e))
    k_gain = jnp.tile(da_k_norm[0], 2)
    pos_meta = jnp.maximum(jnp.arange(META_ROWS, dtype=jnp.int32) - (META_ROWS - N_META), 0)
    tab_main = _rope_tables(jnp.arange(seq, dtype=jnp.int32) + N_META)
    q_hat = _qk_prep(proj, COL_DA_Q, tab_main, q_gain, ATT_BLOCK, seq, 4, False)
    kt = _qk_prep(proj, COL_DA_K, tab_main, k_gain, ATT_BLOCK, seq, 4, True)
    kt_meta = _qk_prep(proj_meta, META_COL_K, _rope_tables(pos_meta), k_gain, META_ROWS, META_ROWS,
                       4, True)
    lam_params = jnp.stack([da_lambda_q1[0], da_lambda_k1[0], da_lambda_q2[0], da_lambda_k2[0]])
    y_a = _attention(q_hat, kt, proj, kt_meta, proj_meta, lam_params,
                     da_subln[0].reshape(1, HEAD_W), batch, seq, ATT_BLOCK)

    s0 = _hg_meta_state(proj_meta, hg_lower_bound)
    y_b = _hgrn2(proj, s0, hg_lower_bound, hg_out_norm[0].reshape(1, HEAD_W), batch, seq, 512)

    merged = _gated_merge(y_a, y_b, w_up_a[0], w_up_b[0], proj, 1024, tn)
    h1 = _matmul_stream(merged, w_out[0], tm=1024, tn=512, out_dtype=F32, res=xf,
                        name="out_proj")

    v = _rmsnorm(h1, norm_mlp[0], 256)
    hid = _matmul_stream(v, w_ff1[0], tm=1024, tn=1024, out_dtype=BF16, relu2=True, name="ff1")
    out = _matmul(hid, w_ff2[0].astype(BF16), tm=1024, tn=1024, tk=4096, out_dtype=F32, res=h1,
                  name="ff2")
    return out.reshape(batch, seq, d)
```

```python
import functools
import math

import jax
import jax.numpy as jnp
from jax import lax
from jax.experimental import pallas as pl
from jax.experimental.pallas import tpu as pltpu

F32 = jnp.float32
BF16 = jnp.bfloat16

D_MODEL = 4096
N_META = 16
HEADS = 16
HEAD_W = 128
QK_DIM = 64
ROPE_DIMS = 16
ROPE_THETA = 500000.0
HG_CHUNK = 64
META_ROWS = 128
ATT_BLOCK = 512
EPS = 1e-6
NEG_INF = -1e30
LAM_INIT = 0.8 - 0.6 * math.exp(-0.3 * 0)

COL_DA_Q, COL_DA_K, COL_DA_V = 0, 16, 32
COL_HG_Q, COL_HG_F, COL_HG_I, COL_HG_G = 48, 64, 80, 96
COL_GATE_A, COL_GATE_B = 112 * HEAD_W, 144 * HEAD_W
IN_WIDTH = 176 * HEAD_W
META_COL_K, META_COL_V, META_COL_F, META_COL_I = 0, 16, 32, 48
META_WIDTH = 64 * HEAD_W

V7X_VMEM_LIMIT_CAP = 58 * 2**20


def _params(semantics, block_bytes, extra_bytes=0):
    need = 2 * block_bytes + extra_bytes + (4 << 20)
    return pltpu.CompilerParams(dimension_semantics=semantics,
                                vmem_limit_bytes=min(max(need, 16 << 20), V7X_VMEM_LIMIT_CAP))


def _rmsnorm_kernel(x_ref, g_ref, o_ref):
    x = x_ref[...]
    r = lax.rsqrt(jnp.mean(x * x, axis=-1, keepdims=True) + EPS)
    o_ref[...] = ((x * r) * g_ref[...]).astype(o_ref.dtype)


def _rmsnorm(x, g, tm):
    m, d = x.shape
    return pl.pallas_call(
        _rmsnorm_kernel,
        grid=(m // tm,),
        in_specs=[pl.BlockSpec((tm, d), lambda i: (i, 0)),
                  pl.BlockSpec((1, d), lambda i: (0, 0))],
        out_specs=pl.BlockSpec((tm, d), lambda i: (i, 0)),
        out_shape=jax.ShapeDtypeStruct((m, d), BF16),
        compiler_params=_params(("parallel",), tm * d * 6),
        name="rmsnorm",
    )(x, g.reshape(1, d))


def _mm_ktiled_kernel(a_ref, w_ref, res_ref, o_ref):
    part = jnp.dot(a_ref[...], w_ref[...], preferred_element_type=F32)
    k = pl.program_id(2)

    @pl.when(k == 0)
    def _():
        o_ref[...] = part + res_ref[...]

    @pl.when(k > 0)
    def _():
        o_ref[...] += part


def _matmul_ktiled(a, w, res, *, tm, tn, tk, name):
    m, kdim = a.shape
    n = w.shape[1]
    tm = min(tm, m)
    block_bytes = tm * tk * 2 + tk * tn * 2 + 2 * tm * tn * 4
    return pl.pallas_call(
        _mm_ktiled_kernel,
        grid=(m // tm, n // tn, kdim // tk),
        in_specs=[pl.BlockSpec((tm, tk), lambda i, j, k: (i, k)),
                  pl.BlockSpec((tk, tn), lambda i, j, k: (k, j)),
                  pl.BlockSpec((tm, tn), lambda i, j, k: (i, j))],
        out_specs=pl.BlockSpec((tm, tn), lambda i, j, k: (i, j)),
        out_shape=jax.ShapeDtypeStruct((m, n), F32),
        compiler_params=_params(("parallel", "parallel", "arbitrary"), block_bytes,
                                2 * tm * tn * 4),
        name=name,
    )(a, w, res)


def _mm_wcast_kernel(*refs, relu2, has_res):
    a_ref, w_ref = refs[0], refs[1]
    res_ref = refs[2] if has_res else None
    o_ref, wb_ref = refs[2 + has_res], refs[3 + has_res]

    @pl.when(pl.program_id(1) == 0)
    def _():
        wb_ref[...] = w_ref[...].astype(BF16)

    acc = jnp.dot(a_ref[...], wb_ref[...], preferred_element_type=F32)
    if relu2:
        acc = jnp.square(jnp.maximum(acc, 0.0))
    if has_res:
        acc = acc + res_ref[...]
    o_ref[...] = acc.astype(o_ref.dtype)


def _matmul_wcast(a, w, *, tm, tn, out_dtype, relu2=False, res=None, name, w_cols=None):
    m, kdim = a.shape
    n, w_col = (w.shape[1], lambda j: j) if w_cols is None else w_cols
    tm = min(tm, m)
    in_specs = [pl.BlockSpec((tm, kdim), lambda j, i: (i, 0)),
                pl.BlockSpec((kdim, tn), lambda j, i: (0, w_col(j)))]
    args = [a, w]
    out_bytes = jnp.dtype(out_dtype).itemsize
    block_bytes = tm * kdim * 2 + kdim * tn * 4 + tm * tn * out_bytes
    if res is not None:
        in_specs.append(pl.BlockSpec((tm, tn), lambda j, i: (i, j)))
        args.append(res)
        block_bytes += tm * tn * 4
    return pl.pallas_call(
        functools.partial(_mm_wcast_kernel, relu2=relu2, has_res=res is not None),
        grid=(n // tn, m // tm),
        in_specs=in_specs,
        out_specs=pl.BlockSpec((tm, tn), lambda j, i: (i, j)),
        out_shape=jax.ShapeDtypeStruct((m, n), out_dtype),
        scratch_shapes=[pltpu.VMEM((kdim, tn), BF16)],
        compiler_params=_params(("arbitrary", "arbitrary"), block_bytes,
                                kdim * tn * 2 + 2 * tm * tn * 4),
        name=name,
    )(*args)


def _mm_stream_kernel(*refs, relu2, has_res, has_side, n_col, n_row):
    a_ref, w_hbm = refs[0], refs[1]
    res_ref = refs[2] if has_res else None
    n_in = 2 + has_res + has_side
    o_ref = refs[n_in]
    wb_ref, stage_ref, sem = refs[n_in + 1 + has_side:n_in + 4 + has_side]
    j, i = pl.program_id(0), pl.program_id(1)
    ck, tn = stage_ref.shape[1], stage_ref.shape[2]

    def chunk_copy(col, c, slot):
        return pltpu.make_async_copy(
            w_hbm.at[pl.ds(pl.multiple_of(c * ck, ck), ck), pl.ds(pl.multiple_of(col * tn, tn), tn)],
            stage_ref.at[slot], sem.at[slot])

    def lookahead_col(jj):
        return jnp.minimum(jj + 1, n_col - 1)

    @pl.when((j == 0) & (i == 0))
    def _():
        for c in range(n_row):
            cp = chunk_copy(0, c, c % 2)
            cp.start()
            cp.wait()
            wb_ref[0, c * ck:(c + 1) * ck, :] = stage_ref[c % 2].astype(BF16)
        chunk_copy(lookahead_col(0), 0, 0).start()

    acc = jnp.dot(a_ref[...], wb_ref[j % 2], preferred_element_type=F32)
    if relu2:
        acc = jnp.square(jnp.maximum(acc, 0.0))
    if has_res:
        acc = acc + res_ref[...]
    o_ref[...] = acc.astype(o_ref.dtype)

    chunk_copy(lookahead_col(j), i, i % 2).wait()
    wb_ref[(j + 1) % 2, pl.ds(pl.multiple_of(i * ck, ck), ck), :] = stage_ref[i % 2].astype(BF16)

    @pl.when((j < n_col - 1) | (i < n_row - 1))
    def _():
        wrap = i == n_row - 1
        chunk_copy(lookahead_col(jnp.where(wrap, j + 1, j)), jnp.where(wrap, 0, i + 1),
                   (i + 1) % 2).start()

    if has_side:
        side_hbm, side_out_hbm = refs[n_in - 1], refs[n_in + 1]
        in_buf, out_buf, in_sem, out_sem = refs[n_in + 5:n_in + 9]
        rs = in_buf.shape[1]
        t, n_steps = j * n_row + i, n_col * n_row

        def side_in(tt):
            return pltpu.make_async_copy(side_hbm.at[pl.ds(pl.multiple_of(tt * rs, rs), rs), :],
                                         in_buf.at[tt % 2], in_sem.at[tt % 2])

        def side_out(tt):
            return pltpu.make_async_copy(out_buf.at[tt % 2],
                                         side_out_hbm.at[pl.ds(pl.multiple_of(tt * rs, rs), rs), :],
                                         out_sem.at[tt % 2])

        @pl.when(t == 0)
        def _():
            side_in(t).start()

        @pl.when(t + 1 < n_steps)
        def _():
            side_in(t + 1).start()

        side_in(t).wait()

        @pl.when(t >= 2)
        def _():
            side_out(t - 2).wait()

        out_buf[t % 2] = in_buf[t % 2].astype(BF16)
        side_out(t).start()

        @pl.when(t == n_steps - 1)
        def _():
            side_out(t - 1).wait()
            side_out(t).wait()


def _matmul_stream(a, w, *, tm, tn, out_dtype, relu2=False, res=None, side=None, name):
    m, kdim = a.shape
    n = w.shape[1]
    n_row, n_col = m // tm, n // tn
    assert n_row % 2 == 0 and n_col >= 2 and kdim % n_row == 0
    ck = kdim // n_row
    in_specs = [pl.BlockSpec((tm, kdim), lambda j, i: (i, 0)),
                pl.BlockSpec(memory_space=pl.ANY)]
    args = [a, w]
    out_bytes = jnp.dtype(out_dtype).itemsize
    block_bytes = tm * kdim * 2 + tm * tn * out_bytes
    if res is not None:
        in_specs.append(pl.BlockSpec((tm, tn), lambda j, i: (i, j)))
        args.append(res)
        block_bytes += tm * tn * 4
    out_specs = pl.BlockSpec((tm, tn), lambda j, i: (i, j))
    out_shape = jax.ShapeDtypeStruct((m, n), out_dtype)
    scratch = [pltpu.VMEM((2, kdim, tn), BF16), pltpu.VMEM((2, ck, tn), F32),
               pltpu.SemaphoreType.DMA((2,))]
    scratch_bytes = 2 * kdim * tn * 2 + 2 * ck * tn * 4
    if side is not None:
        rows, cols = side.shape
        assert rows % (n_row * n_col) == 0
        rs = rows // (n_row * n_col)
        in_specs.append(pl.BlockSpec(memory_space=pl.ANY))
        args.append(side)
        out_specs = [out_specs, pl.BlockSpec(memory_space=pl.ANY)]
        out_shape = [out_shape, jax.ShapeDtypeStruct(side.shape, BF16)]
        scratch += [pltpu.VMEM((2, rs, cols), F32), pltpu.VMEM((2, rs, cols), BF16),
                    pltpu.SemaphoreType.DMA((2,)), pltpu.SemaphoreType.DMA((2,))]
        scratch_bytes += 2 * rs * cols * 6
    return pl.pallas_call(
        functools.partial(_mm_stream_kernel, relu2=relu2, has_res=res is not None,
                          has_side=side is not None, n_col=n_col, n_row=n_row),
        grid=(n_col, n_row),
        in_specs=in_specs,
        out_specs=out_specs,
        out_shape=out_shape,
        scratch_shapes=scratch,
        compiler_params=_params(("arbitrary", "arbitrary"), block_bytes,
                                scratch_bytes + tm * tn * 4),
        name=name,
    )(*args)


def _split_terms(x, n):
    terms = []
    for _ in range(n - 1):
        t = x.astype(BF16)
        terms.append(t)
        x = x - t.astype(F32)
    terms.append(x.astype(BF16))
    return terms


def _qk_prep_kernel(x_ref, cos_ref, sin_lo_ref, sin_hi_ref, g_ref, ones_ref, o_ref, *,
                    heads_per_block, transpose_out):
    g, cos, sin_lo, sin_hi = g_ref[...], cos_ref[...], sin_lo_ref[...], sin_hi_ref[...]
    for hh in range(heads_per_block):
        sl = slice(hh * HEAD_W, (hh + 1) * HEAD_W)
        x = x_ref[:, sl].astype(F32)
        sq = jnp.concatenate(_split_terms(x * x, 2), axis=1)
        ss = jnp.dot(sq, ones_ref[...], preferred_element_type=F32)
        r = lax.rsqrt(ss * (1.0 / QK_DIM) + EPS)
        y = (x * r) * g
        y = (y * cos + pltpu.roll(y, HEAD_W - ROPE_DIMS // 2, 1) * sin_lo
             + pltpu.roll(y, ROPE_DIMS // 2, 1) * sin_hi)
        if transpose_out:
            o_ref[0, hh, 0] = y.T.astype(o_ref.dtype)
        else:
            o_ref[:, sl] = y.astype(o_ref.dtype)


def _qk_prep(proj, col0, tables, gain, tm, rows_per_seq, heads_per_block, transpose_out):
    m = proj.shape[0]
    nblk = rows_per_seq // tm
    tw = heads_per_block * HEAD_W
    lane_group = jnp.arange(HEAD_W) // QK_DIM
    group_ones = jnp.tile((lane_group[:, None] == lane_group[None, :]).astype(BF16), (2, 1))
    tab_spec = pl.BlockSpec((tm, HEAD_W), lambda i, j: (i % nblk, 0))
    if transpose_out:
        out_spec = pl.BlockSpec((1, heads_per_block, 1, HEAD_W, tm),
                                lambda i, j: (i // nblk, j, i % nblk, 0, 0))
        out_shape = jax.ShapeDtypeStruct((m // rows_per_seq, HEADS, nblk, HEAD_W, tm), BF16)
    else:
        out_spec = pl.BlockSpec((tm, tw), lambda i, j: (i, j))
        out_shape = jax.ShapeDtypeStruct((m, HEADS * HEAD_W), BF16)
    return pl.pallas_call(
        functools.partial(_qk_prep_kernel, heads_per_block=heads_per_block,
                          transpose_out=transpose_out),
        grid=(m // tm, HEADS // heads_per_block),
        in_specs=[pl.BlockSpec((tm, tw), lambda i, j: (i, col0 * HEAD_W // tw + j)),
                  tab_spec, tab_spec, tab_spec,
                  pl.BlockSpec((1, HEAD_W), lambda i, j: (0, 0)),
                  pl.BlockSpec((2 * HEAD_W, HEAD_W), lambda i, j: (0, 0))],
        out_specs=out_spec,
        out_shape=out_shape,
        compiler_params=_params(("parallel", "parallel"), tm * tw * 4 + 3 * tm * HEAD_W * 4,
                                tm * HEAD_W * 48),
        name="k_prep" if transpose_out else "q_prep",
    )(proj, *tables, gain.reshape(1, HEAD_W), group_ones)


def _rope_tables(pos):
    half = ROPE_DIMS // 2
    inv_freq = ROPE_THETA ** (-(jnp.arange(half, dtype=F32) * 2.0) / ROPE_DIMS)
    ang = pos.astype(F32)[:, None] * inv_freq[None, :]
    cos, sin = jnp.cos(ang), jnp.sin(ang)
    n = pos.shape[0]
    pad = QK_DIM - ROPE_DIMS
    ones, zeros, zh = jnp.ones((n, pad), F32), jnp.zeros((n, pad), F32), jnp.zeros((n, half), F32)
    cos_t = jnp.concatenate([cos, cos, ones], axis=1)
    sin_lo = jnp.concatenate([-sin, zh, zeros], axis=1)
    sin_hi = jnp.concatenate([zh, sin, zeros], axis=1)
    return tuple(jnp.tile(t, (1, 2)) for t in (cos_t, sin_lo, sin_hi))


def _nt_dot(a, b):
    return lax.dot_general(a, b, (((1,), (1,)), ((), ())), preferred_element_type=F32)


def _lane_repeat(x, n):
    return jnp.concatenate([x] * n, axis=1)


def _attn_kernel(q_ref, kt_ref, v_ref, ktm_ref, vm_ref, lam_ref, subln_ref, o_ref,
                 vp_ref, vpm_ref, *, tq, nq):
    ones = jnp.ones((vp_ref.shape[0], HEAD_W), BF16)
    vp_ref[:, :HEAD_W] = v_ref[...]
    vp_ref[:, HEAD_W:] = ones
    vpm_ref[:, :HEAD_W] = vm_ref[...]
    vpm_ref[:, HEAD_W:] = ones[:META_ROWS]

    lp = lam_ref[...]
    lam = (jnp.exp(jnp.sum(lp[0:1] * lp[1:2], axis=-1, keepdims=True))
           - jnp.exp(jnp.sum(lp[2:3] * lp[3:4], axis=-1, keepdims=True)) + LAM_INIT)
    lane = lax.broadcasted_iota(jnp.int32, (tq, HEAD_W), 1)
    col_m = lax.broadcasted_iota(jnp.int32, (tq, META_ROWS), 1)
    row = lax.broadcasted_iota(jnp.int32, (tq, tq), 0)
    col = lax.broadcasted_iota(jnp.int32, (tq, tq), 1)

    def q_maps(qv):
        q = q_ref[qv * tq:(qv + 1) * tq, :]
        zero = jnp.zeros_like(q)
        return (jnp.where(lane < QK_DIM, q, zero),
                jnp.where(lane >= QK_DIM, q, zero))

    def scores(task):
        qv, j = task
        kt = ktm_ref[0, 0, 0] if j is None else kt_ref[0, 0, j]
        return [jnp.dot(qc, kt, preferred_element_type=F32) for qc in q_maps(qv)]

    def consume(state, s_maps, vp, mask):
        out = []
        for (m_old, acc), s in zip(state, s_maps):
            if mask is not None:
                s = jnp.where(mask, s, NEG_INF)
            m_new = jnp.maximum(m_old, jnp.max(s, axis=-1, keepdims=True))
            p = jnp.exp2((s - _lane_repeat(m_new, s.shape[1] // HEAD_W)).astype(BF16))
            alpha = jnp.exp2(m_old - m_new)
            acc = (_lane_repeat(alpha, 2) * acc
                   + jnp.dot(p, vp, preferred_element_type=F32))
            out.append((m_new, acc))
        return out

    def finalize(qv, state):
        a1, a2 = state[0][1], state[1][1]
        o = (a1[:, :HEAD_W] / a1[:, HEAD_W:HEAD_W + 1]
             - lam * (a2[:, :HEAD_W] / a2[:, HEAD_W:HEAD_W + 1]))
        r = lax.rsqrt(jnp.mean(o * o, axis=-1, keepdims=True) + EPS)
        o_ref[qv * tq:(qv + 1) * tq, :] = (
            ((o * r) * subln_ref[...]) * (1.0 - LAM_INIT)).astype(o_ref.dtype)

    tasks = [(qv, j) for qv in range(nq) for j in [None] + list(range(qv + 1))]
    s_next = scores(tasks[0])
    state = None
    for t, (qv, j) in enumerate(tasks):
        s_cur = s_next
        if t + 1 < len(tasks):
            s_next = scores(tasks[t + 1])
        if j is None:
            state = [(jnp.full((tq, HEAD_W), NEG_INF, F32), jnp.zeros((tq, 2 * HEAD_W), F32))] * 2
            state = consume(state, s_cur, vpm_ref[...], col_m >= META_ROWS - N_META)
        else:
            state = consume(state, s_cur, vp_ref[j * tq:(j + 1) * tq, :],
                            col <= row if j == qv else None)
            if j == qv:
                finalize(qv, state)


def _attention(q, kt, proj, kt_meta, proj_meta, lam_params, subln, batch, seq, tq):
    nq = seq // tq
    kernel = functools.partial(_attn_kernel, tq=tq, nq=nq)
    blk = 4 * seq * HEAD_W * 2 + 2 * META_ROWS * HEAD_W * 2
    scratch_bytes = (seq + META_ROWS) * 2 * HEAD_W * 2
    return pl.pallas_call(
        kernel,
        grid=(batch, HEADS),
        in_specs=[pl.BlockSpec((seq, HEAD_W), lambda b, h: (b, h)),
                  pl.BlockSpec((1, 1, nq, HEAD_W, tq), lambda b, h: (b, h, 0, 0, 0)),
                  pl.BlockSpec((seq, HEAD_W), lambda b, h: (b, COL_DA_V + h)),
                  pl.BlockSpec((1, 1, 1, HEAD_W, META_ROWS), lambda b, h: (0, h, 0, 0, 0)),
                  pl.BlockSpec((META_ROWS, HEAD_W), lambda b, h: (0, META_COL_V + h)),
                  pl.BlockSpec((4, QK_DIM), lambda b, h: (0, 0)),
                  pl.BlockSpec((1, HEAD_W), lambda b, h: (0, 0))],
        out_specs=pl.BlockSpec((seq, HEAD_W), lambda b, h: (b, h)),
        out_shape=jax.ShapeDtypeStruct((batch * seq, HEADS * HEAD_W), BF16),
        scratch_shapes=[pltpu.VMEM((seq, 2 * HEAD_W), BF16),
                        pltpu.VMEM((META_ROWS, 2 * HEAD_W), BF16)],
        compiler_params=_params(("parallel", "parallel"), blk, scratch_bytes + 16 * tq * tq * 4),
        name="diff_attention",
    )(q, kt, proj, kt_meta, proj_meta, lam_params, subln)


def _lower_bound(lb_ref):
    x = lb_ref[...]
    e = jnp.exp(x - jnp.max(x, axis=0, keepdims=True))
    return e[0:1] / jnp.sum(e, axis=0, keepdims=True)


def _chunk_tri(rows):
    r = jnp.arange(rows)
    same_chunk = r[:, None] // HG_CHUNK == r[None, :] // HG_CHUNK
    return (same_chunk & (r[None, :] <= r[:, None])).astype(BF16)


def _chunk_cumsum(tri, g):
    parts = jnp.concatenate(_split_terms(g, 3), axis=1)
    s = jnp.dot(tri, parts, preferred_element_type=F32)
    return s[:, :HEAD_W] + s[:, HEAD_W:2 * HEAD_W] + s[:, 2 * HEAD_W:]


def _hg_block(q, fl, v, lb, tri, st, valid=None):
    c = HG_CHUNK
    n = q.shape[0] // c
    f = lb + (1.0 - lb) * jax.nn.sigmoid(fl)
    g = jnp.log(f)
    kk = 1.0 - f
    if valid is not None:
        g = jnp.where(valid, g, 0.0)
        kk = jnp.where(valid, kk, 0.0)
        v = jnp.where(valid, v, jnp.zeros_like(v))
    b = _chunk_cumsum(tri, g).reshape(n, c, HEAD_W)
    q3, k3, v3 = q.reshape(n, c, HEAD_W), kk.reshape(n, c, HEAD_W), v.reshape(n, c, HEAD_W)
    b_mid = b[:, c // 2 - 1:c // 2, :]
    b_last = b[:, c - 1:c, :]
    q_in = (q3 * jnp.exp(b - b_mid)).astype(BF16)
    k_in = (k3 * jnp.exp(b_mid - b)).astype(BF16)
    k_dec = (k3 * jnp.exp(b_last - b)).astype(BF16)
    q_ex = (q3 * jnp.exp(b)).astype(BF16)
    decay = jnp.exp(b_last)
    row = lax.broadcasted_iota(jnp.int32, (n, c, c), 1)
    col = lax.broadcasted_iota(jnp.int32, (n, c, c), 2)
    a = lax.dot_general(q_in, k_in, (((2,), (2,)), ((0,), (0,))), preferred_element_type=F32)
    a = jnp.where(col <= row, a, 0.0).astype(BF16)
    o_intra = lax.dot_general(a, v3, (((2,), (1,)), ((0,), (0,))), preferred_element_type=F32)
    u_t = [lax.dot_general(v3[i], k_dec[i], (((0,), (0,)), ((), ())), preferred_element_type=F32)
           for i in range(n)]
    outs = []
    for i in range(n):
        outs.append(o_intra[i] + _nt_dot(q_ex[i], st.astype(BF16)))
        st = st * decay[i] + u_t[i]
    return outs, st


def _hg_meta_kernel(f_ref, i_ref, tri_ref, lb_ref, s_ref):
    row = lax.broadcasted_iota(jnp.int32, (HG_CHUNK, HEAD_W), 0)
    _, st = _hg_block(jnp.zeros((HG_CHUNK, HEAD_W), F32), f_ref[...].astype(F32), i_ref[...],
                      _lower_bound(lb_ref), tri_ref[...], jnp.zeros((HEAD_W, HEAD_W), F32),
                      valid=row >= HG_CHUNK - N_META)
    s_ref[0] = st


def _hg_meta_state(proj_meta, lower_bound):
    blk = META_ROWS // HG_CHUNK - 1
    return pl.pallas_call(
        _hg_meta_kernel,
        grid=(HEADS,),
        in_specs=[pl.BlockSpec((HG_CHUNK, HEAD_W), lambda h: (blk, META_COL_F + h)),
                  pl.BlockSpec((HG_CHUNK, HEAD_W), lambda h: (blk, META_COL_I + h)),
                  pl.BlockSpec((HG_CHUNK, HG_CHUNK), lambda h: (0, 0)),
                  pl.BlockSpec((lower_bound.shape[0], HEAD_W), lambda h: (0, h))],
        out_specs=pl.BlockSpec((1, HEAD_W, HEAD_W), lambda h: (h, 0, 0)),
        out_shape=jax.ShapeDtypeStruct((HEADS, HEAD_W, HEAD_W), F32),
        compiler_params=_params(("parallel",), 1 << 20),
        name="hgrn2_meta_state",
    )(proj_meta, proj_meta, _chunk_tri(HG_CHUNK), lower_bound)


def _hgrn_kernel(q_ref, f_ref, i_ref, g_ref, tri_ref, lb_ref, s0_ref, on_ref, o_ref, st_ref):
    @pl.when(pl.program_id(2) == 0)
    def _():
        st_ref[...] = s0_ref[0]

    outs, st = _hg_block(q_ref[...].astype(F32), f_ref[...].astype(F32), i_ref[...],
                         _lower_bound(lb_ref), tri_ref[...], st_ref[...])
    st_ref[...] = st
    for i, o in enumerate(outs):
        sl = pl.ds(i * HG_CHUNK, HG_CHUNK)
        r = lax.rsqrt(jnp.mean(o * o, axis=-1, keepdims=True) + EPS)
        y = ((o * r) * on_ref[...]) * jax.nn.sigmoid(g_ref[sl, :].astype(F32))
        o_ref[sl, :] = y.astype(o_ref.dtype)


def _hgrn2(proj, s0, lower_bound, out_norm, batch, seq, tb):
    nt = seq // tb

    def col(off):
        return pl.BlockSpec((tb, HEAD_W), lambda b, h, t: (b * nt + t, off + h))

    return pl.pallas_call(
        _hgrn_kernel,
        grid=(batch, HEADS, nt),
        in_specs=[col(COL_HG_Q), col(COL_HG_F), col(COL_HG_I), col(COL_HG_G),
                  pl.BlockSpec((tb, tb), lambda b, h, t: (0, 0)),
                  pl.BlockSpec((lower_bound.shape[0], HEAD_W), lambda b, h, t: (0, h)),
                  pl.BlockSpec((1, HEAD_W, HEAD_W), lambda b, h, t: (h, 0, 0)),
                  pl.BlockSpec((1, HEAD_W), lambda b, h, t: (0, 0))],
        out_specs=pl.BlockSpec((tb, HEAD_W), lambda b, h, t: (b * nt + t, h)),
        out_shape=jax.ShapeDtypeStruct((batch * seq, HEADS * HEAD_W), BF16),
        scratch_shapes=[pltpu.VMEM((HEAD_W, HEAD_W), F32)],
        compiler_params=_params(("parallel", "parallel", "arbitrary"),
                                5 * tb * HEAD_W * 2 + tb * tb * 2, 24 * tb * HEAD_W * 4),
        name="hgrn2",
    )(proj, proj, proj, proj, _chunk_tri(tb), lower_bound, s0, out_norm)


def _merge_kernel(ya_ref, yb_ref, wa_ref, wb_ref, ga_ref, gb_ref, o_ref, wa_bf_ref, wb_bf_ref):
    @pl.when(pl.program_id(1) == 0)
    def _():
        wa_bf_ref[...] = wa_ref[...].astype(BF16)
        wb_bf_ref[...] = wb_ref[...].astype(BF16)

    a = jnp.dot(ya_ref[...], wa_bf_ref[...], preferred_element_type=F32)
    b = jnp.dot(yb_ref[...], wb_bf_ref[...], preferred_element_type=F32)
    o = (jax.nn.sigmoid(ga_ref[...].astype(F32)) * a + jax.nn.sigmoid(gb_ref[...].astype(F32)) * b)
    o_ref[...] = o.astype(o_ref.dtype)


def _gated_merge(y_a, y_b, w_a, w_b, proj, tm, tn):
    m, kdim = y_a.shape
    n = w_a.shape[1]
    tm = min(tm, m)
    blk = 2 * tm * kdim * 2 + 2 * kdim * tn * 4 + 3 * tm * tn * 2
    return pl.pallas_call(
        _merge_kernel,
        grid=(n // tn, m // tm),
        in_specs=[pl.BlockSpec((tm, kdim), lambda j, i: (i, 0)),
                  pl.BlockSpec((tm, kdim), lambda j, i: (i, 0)),
                  pl.BlockSpec((kdim, tn), lambda j, i: (0, j)),
                  pl.BlockSpec((kdim, tn), lambda j, i: (0, j)),
                  pl.BlockSpec((tm, tn), lambda j, i: (i, COL_GATE_A // tn + j)),
                  pl.BlockSpec((tm, tn), lambda j, i: (i, COL_GATE_B // tn + j))],
        out_specs=pl.BlockSpec((tm, tn), lambda j, i: (i, j)),
        out_shape=jax.ShapeDtypeStruct((m, n), BF16),
        scratch_shapes=[pltpu.VMEM((kdim, tn), BF16), pltpu.VMEM((kdim, tn), BF16)],
        compiler_params=_params(("arbitrary", "arbitrary"), blk,
                                2 * kdim * tn * 2 + 4 * tm * tn * 4),
        name="gated_merge",
    )(y_a, y_b, w_a, w_b, proj, proj)


def kernel(x, meta_tokens, norm_mix, w_in, da_q_norm, da_k_norm, da_lambda_q1, da_lambda_k1,
           da_lambda_q2, da_lambda_k2, da_subln, hg_lower_bound, hg_out_norm, w_up_a, w_up_b,
           w_out, norm_mlp, w_ff1, w_ff2):
    batch, seq, d = x.shape
    assert d == D_MODEL and w_in.shape[0] == 1 and w_in.shape[2] == IN_WIDTH
    assert meta_tokens.shape == (N_META, D_MODEL) and seq % 512 == 0
    m = batch * seq
    xf = x.reshape(m, d)
    meta_pad = jnp.concatenate([jnp.zeros((META_ROWS - N_META, d), F32), meta_tokens.astype(F32)])

    tn = 512
    u = _rmsnorm(xf, norm_mix[0], 256)
    u_meta = _rmsnorm(meta_pad, norm_mix[0], META_ROWS)
    proj = _matmul_stream(u, w_in[0], tm=1024, tn=1024, out_dtype=BF16, name="in_proj")
    kv_blocks = 2 * HEADS * HEAD_W // tn

    def meta_col(j):
        return jnp.where(j < kv_blocks, COL_DA_K * HEAD_W // tn + j,
                         COL_HG_F * HEAD_W // tn + j - kv_blocks)

    proj_meta = _matmul_wcast(u_meta, w_in[0], tm=META_ROWS, tn=tn, out_dtype=BF16,
                              name="in_proj_meta", w_cols=(META_WIDTH, meta_col))

    q_gain = jnp.tile(da_q_norm[0], 2) * (QK_DIM ** -0.5 * math.log2(math.e))
    k_gain = jnp.tile(da_k_norm[0], 2)
    pos_meta = jnp.maximum(jnp.arange(META_ROWS, dtype=jnp.int32) - (META_ROWS - N_META), 0)
    tab_main = _rope_tables(jnp.arange(seq, dtype=jnp.int32) + N_META)
    q_hat = _qk_prep(proj, COL_DA_Q, tab_main, q_gain, ATT_BLOCK, seq, 4, False)
    kt = _qk_prep(proj, COL_DA_K, tab_main, k_gain, ATT_BLOCK, seq, 4, True)
    kt_meta = _qk_prep(proj_meta, META_COL_K, _rope_tables(pos_meta), k_gain, META_ROWS, META_ROWS,
                       4, True)
    lam_params = jnp.stack([da_lambda_q1[0], da_lambda_k1[0], da_lambda_q2[0], da_lambda_k2[0]])
    y_a = _attention(q_hat, kt, proj, kt_meta, proj_meta, lam_params,
                     da_subln[0].reshape(1, HEAD_W), batch, seq, ATT_BLOCK)

    s0 = _hg_meta_state(proj_meta, hg_lower_bound)
    y_b = _hgrn2(proj, s0, hg_lower_bound, hg_out_norm[0].reshape(1, HEAD_W), batch, seq, 512)

    merged = _gated_merge(y_a, y_b, w_up_a[0], w_up_b[0], proj, 1024, tn)
    h1 = _matmul_stream(merged, w_out[0], tm=1024, tn=512, out_dtype=F32, res=xf,
                        name="out_proj")

    v = _rmsnorm(h1, norm_mlp[0], 256)
    hid, w_ff2_b = _matmul_stream(v, w_ff1[0], tm=1024, tn=1024, out_dtype=BF16, relu2=True,
                                  side=w_ff2[0], name="ff1")
    out = _matmul_ktiled(hid, w_ff2_b, h1, tm=1024, tn=1024, tk=4096, name="ff2")
    return out.reshape(batch, seq, d)
```

```python
import functools
import math

import jax
import jax.numpy as jnp
from jax import lax
from jax.experimental import pallas as pl
from jax.experimental.pallas import tpu as pltpu

F32 = jnp.float32
BF16 = jnp.bfloat16

D_MODEL = 4096
N_META = 16
HEADS = 16
HEAD_W = 128
QK_DIM = 64
ROPE_DIMS = 16
ROPE_THETA = 500000.0
HG_CHUNK = 64
META_ROWS = 128
ATT_BLOCK = 512
EPS = 1e-6
NEG_INF = -1e30
LAM_INIT = 0.8 - 0.6 * math.exp(-0.3 * 0)

COL_DA_Q, COL_DA_K, COL_DA_V = 0, 16, 32
COL_HG_Q, COL_HG_F, COL_HG_I, COL_HG_G = 48, 64, 80, 96
COL_GATE_A, COL_GATE_B = 112 * HEAD_W, 144 * HEAD_W
IN_WIDTH = 176 * HEAD_W
META_COL_K, META_COL_V, META_COL_F, META_COL_I = 0, 16, 32, 48
META_WIDTH = 64 * HEAD_W

V7X_VMEM_LIMIT_CAP = 58 * 2**20


def _params(semantics, block_bytes, extra_bytes=0):
    need = 2 * block_bytes + extra_bytes + (4 << 20)
    return pltpu.CompilerParams(dimension_semantics=semantics,
                                vmem_limit_bytes=min(max(need, 16 << 20), V7X_VMEM_LIMIT_CAP))


def _rmsnorm_kernel(x_ref, g_ref, o_ref):
    x = x_ref[...]
    r = lax.rsqrt(jnp.mean(x * x, axis=-1, keepdims=True) + EPS)
    o_ref[...] = ((x * r) * g_ref[...]).astype(o_ref.dtype)


def _rmsnorm(x, g, tm):
    m, d = x.shape
    return pl.pallas_call(
        _rmsnorm_kernel,
        grid=(m // tm,),
        in_specs=[pl.BlockSpec((tm, d), lambda i: (i, 0)),
                  pl.BlockSpec((1, d), lambda i: (0, 0))],
        out_specs=pl.BlockSpec((tm, d), lambda i: (i, 0)),
        out_shape=jax.ShapeDtypeStruct((m, d), BF16),
        compiler_params=_params(("parallel",), tm * d * 6),
        name="rmsnorm",
    )(x, g.reshape(1, d))


def _mm_ktiled_kernel(a_ref, w_ref, res_ref, o_ref):
    part = jnp.dot(a_ref[...], w_ref[...], preferred_element_type=F32)
    k = pl.program_id(2)

    @pl.when(k == 0)
    def _():
        o_ref[...] = part + res_ref[...]

    @pl.when(k > 0)
    def _():
        o_ref[...] += part


def _matmul_ktiled(a, w, res, *, tm, tn, tk, name):
    m, kdim = a.shape
    n = w.shape[1]
    tm = min(tm, m)
    block_bytes = tm * tk * 2 + tk * tn * 2 + 2 * tm * tn * 4
    return pl.pallas_call(
        _mm_ktiled_kernel,
        grid=(m // tm, n // tn, kdim // tk),
        in_specs=[pl.BlockSpec((tm, tk), lambda i, j, k: (i, k)),
                  pl.BlockSpec((tk, tn), lambda i, j, k: (k, j)),
                  pl.BlockSpec((tm, tn), lambda i, j, k: (i, j))],
        out_specs=pl.BlockSpec((tm, tn), lambda i, j, k: (i, j)),
        out_shape=jax.ShapeDtypeStruct((m, n), F32),
        compiler_params=_params(("parallel", "parallel", "arbitrary"), block_bytes,
                                2 * tm * tn * 4),
        name=name,
    )(a, w, res)


def _mm_wcast_kernel(*refs, relu2, has_res):
    a_ref, w_ref = refs[0], refs[1]
    res_ref = refs[2] if has_res else None
    o_ref, wb_ref = refs[2 + has_res], refs[3 + has_res]

    @pl.when(pl.program_id(1) == 0)
    def _():
        wb_ref[...] = w_ref[...].astype(BF16)

    acc = jnp.dot(a_ref[...], wb_ref[...], preferred_element_type=F32)
    if relu2:
        acc = jnp.square(jnp.maximum(acc, 0.0))
    if has_res:
        acc = acc + res_ref[...]
    o_ref[...] = acc.astype(o_ref.dtype)


def _matmul_wcast(a, w, *, tm, tn, out_dtype, relu2=False, res=None, name, w_cols=None):
    m, kdim = a.shape
    n, w_col = (w.shape[1], lambda j: j) if w_cols is None else w_cols
    tm = min(tm, m)
    in_specs = [pl.BlockSpec((tm, kdim), lambda j, i: (i, 0)),
                pl.BlockSpec((kdim, tn), lambda j, i: (0, w_col(j)))]
    args = [a, w]
    out_bytes = jnp.dtype(out_dtype).itemsize
    block_bytes = tm * kdim * 2 + kdim * tn * 4 + tm * tn * out_bytes
    if res is not None:
        in_specs.append(pl.BlockSpec((tm, tn), lambda j, i: (i, j)))
        args.append(res)
        block_bytes += tm * tn * 4
    return pl.pallas_call(
        functools.partial(_mm_wcast_kernel, relu2=relu2, has_res=res is not None),
        grid=(n // tn, m // tm),
        in_specs=in_specs,
        out_specs=pl.BlockSpec((tm, tn), lambda j, i: (i, j)),
        out_shape=jax.ShapeDtypeStruct((m, n), out_dtype),
        scratch_shapes=[pltpu.VMEM((kdim, tn), BF16)],
        compiler_params=_params(("arbitrary", "arbitrary"), block_bytes,
                                kdim * tn * 2 + 2 * tm * tn * 4),
        name=name,
    )(*args)


def _mm_stream_kernel(*refs, relu2, has_res, has_side, n_col, n_row):
    a_ref, w_hbm = refs[0], refs[1]
    res_ref = refs[2] if has_res else None
    n_in = 2 + has_res + has_side
    o_ref = refs[n_in]
    wb_ref, stage_ref, sem = refs[n_in + 1 + has_side:n_in + 4 + has_side]
    j, i = pl.program_id(0), pl.program_id(1)
    ck, tn = stage_ref.shape[1], stage_ref.shape[2]

    def chunk_copy(col, c, slot):
        return pltpu.make_async_copy(
            w_hbm.at[pl.ds(pl.multiple_of(c * ck, ck), ck), pl.ds(pl.multiple_of(col * tn, tn), tn)],
            stage_ref.at[slot], sem.at[slot])

    def lookahead_col(jj):
        return jnp.minimum(jj + 1, n_col - 1)

    @pl.when((j == 0) & (i == 0))
    def _():
        for c in range(n_row):
            cp = chunk_copy(0, c, c % 2)
            cp.start()
            cp.wait()
            wb_ref[0, c * ck:(c + 1) * ck, :] = stage_ref[c % 2].astype(BF16)
        chunk_copy(lookahead_col(0), 0, 0).start()

    acc = jnp.dot(a_ref[...], wb_ref[j % 2], preferred_element_type=F32)
    if relu2:
        acc = jnp.square(jnp.maximum(acc, 0.0))
    if has_res:
        acc = acc + res_ref[...]
    o_ref[...] = acc.astype(o_ref.dtype)

    chunk_copy(lookahead_col(j), i, i % 2).wait()
    wb_ref[(j + 1) % 2, pl.ds(pl.multiple_of(i * ck, ck), ck), :] = stage_ref[i % 2].astype(BF16)

    @pl.when((j < n_col - 1) | (i < n_row - 1))
    def _():
        wrap = i == n_row - 1
        chunk_copy(lookahead_col(jnp.where(wrap, j + 1, j)), jnp.where(wrap, 0, i + 1),
                   (i + 1) % 2).start()

    if has_side:
        side_hbm, side_out_hbm = refs[n_in - 1], refs[n_in + 1]
        in_buf, out_buf, in_sem, out_sem = refs[n_in + 5:n_in + 9]
        rs = in_buf.shape[1]
        t, n_steps = j * n_row + i, n_col * n_row

        def side_in(tt):
            return pltpu.make_async_copy(side_hbm.at[pl.ds(pl.multiple_of(tt * rs, rs), rs), :],
                                         in_buf.at[tt % 2], in_sem.at[tt % 2])

        def side_out(tt):
            return pltpu.make_async_copy(out_buf.at[tt % 2],
                                         side_out_hbm.at[pl.ds(pl.multiple_of(tt * rs, rs), rs), :],
                                         out_sem.at[tt % 2])

        @pl.when(t == 0)
        def _():
            side_in(t).start()

        @pl.when(t + 1 < n_steps)
        def _():
            side_in(t + 1).start()

        side_in(t).wait()

        @pl.when(t >= 2)
        def _():
            side_out(t - 2).wait()

        out_buf[t % 2] = in_buf[t % 2].astype(BF16)
        side_out(t).start()

        @pl.when(t == n_steps - 1)
        def _():
            side_out(t - 1).wait()
            side_out(t).wait()


def _matmul_stream(a, w, *, tm, tn, out_dtype, relu2=False, res=None, side=None, name):
    m, kdim = a.shape
    n = w.shape[1]
    n_row, n_col = m // tm, n // tn
    assert n_row % 2 == 0 and n_col >= 2 and kdim % n_row == 0
    ck = kdim // n_row
    in_specs = [pl.BlockSpec((tm, kdim), lambda j, i: (i, 0)),
                pl.BlockSpec(memory_space=pl.ANY)]
    args = [a, w]
    out_bytes = jnp.dtype(out_dtype).itemsize
    block_bytes = tm * kdim * 2 + tm * tn * out_bytes
    if res is not None:
        in_specs.append(pl.BlockSpec((tm, tn), lambda j, i: (i, j)))
        args.append(res)
        block_bytes += tm * tn * 4
    out_specs = pl.BlockSpec((tm, tn), lambda j, i: (i, j))
    out_shape = jax.ShapeDtypeStruct((m, n), out_dtype)
    scratch = [pltpu.VMEM((2, kdim, tn), BF16), pltpu.VMEM((2, ck, tn), F32),
               pltpu.SemaphoreType.DMA((2,))]
    scratch_bytes = 2 * kdim * tn * 2 + 2 * ck * tn * 4
    if side is not None:
        rows, cols = side.shape
        assert rows % (n_row * n_col) == 0
        rs = rows // (n_row * n_col)
        in_specs.append(pl.BlockSpec(memory_space=pl.ANY))
        args.append(side)
        out_specs = [out_specs, pl.BlockSpec(memory_space=pl.ANY)]
        out_shape = [out_shape, jax.ShapeDtypeStruct(side.shape, BF16)]
        scratch += [pltpu.VMEM((2, rs, cols), F32), pltpu.VMEM((2, rs, cols), BF16),
                    pltpu.SemaphoreType.DMA((2,)), pltpu.SemaphoreType.DMA((2,))]
        scratch_bytes += 2 * rs * cols * 6
    return pl.pallas_call(
        functools.partial(_mm_stream_kernel, relu2=relu2, has_res=res is not None,
                          has_side=side is not None, n_col=n_col, n_row=n_row),
        grid=(n_col, n_row),
        in_specs=in_specs,
        out_specs=out_specs,
        out_shape=out_shape,
        scratch_shapes=scratch,
        compiler_params=_params(("arbitrary", "arbitrary"), block_bytes,
                                scratch_bytes + tm * tn * 4),
        name=name,
    )(*args)


def _split_terms(x, n):
    terms = []
    for _ in range(n - 1):
        t = x.astype(BF16)
        terms.append(t)
        x = x - t.astype(F32)
    terms.append(x.astype(BF16))
    return terms


def _qk_prep_kernel(x_ref, cos_ref, sin_lo_ref, sin_hi_ref, g_ref, ones_ref, o_ref, *,
                    heads_per_block, transpose_out):
    g, cos, sin_lo, sin_hi = g_ref[...], cos_ref[...], sin_lo_ref[...], sin_hi_ref[...]
    for hh in range(heads_per_block):
        sl = slice(hh * HEAD_W, (hh + 1) * HEAD_W)
        x = x_ref[:, sl].astype(F32)
        ss = jnp.dot((x * x).astype(BF16), ones_ref[...], preferred_element_type=F32)
        r = lax.rsqrt(ss * (1.0 / QK_DIM) + EPS)
        y = (x * r) * g
        y = (y * cos + pltpu.roll(y, HEAD_W - ROPE_DIMS // 2, 1) * sin_lo
             + pltpu.roll(y, ROPE_DIMS // 2, 1) * sin_hi)
        if transpose_out:
            o_ref[0, hh, 0] = y.T.astype(o_ref.dtype)
        else:
            o_ref[:, sl] = y.astype(o_ref.dtype)


def _qk_prep(proj, col0, tables, gain, tm, rows_per_seq, heads_per_block, transpose_out):
    m = proj.shape[0]
    nblk = rows_per_seq // tm
    tw = heads_per_block * HEAD_W
    lane_group = jnp.arange(HEAD_W) // QK_DIM
    group_ones = (lane_group[:, None] == lane_group[None, :]).astype(BF16)
    tab_spec = pl.BlockSpec((tm, HEAD_W), lambda i, j: (i % nblk, 0))
    if transpose_out:
        out_spec = pl.BlockSpec((1, heads_per_block, 1, HEAD_W, tm),
                                lambda i, j: (i // nblk, j, i % nblk, 0, 0))
        out_shape = jax.ShapeDtypeStruct((m // rows_per_seq, HEADS, nblk, HEAD_W, tm), BF16)
    else:
        out_spec = pl.BlockSpec((tm, tw), lambda i, j: (i, j))
        out_shape = jax.ShapeDtypeStruct((m, HEADS * HEAD_W), BF16)
    return pl.pallas_call(
        functools.partial(_qk_prep_kernel, heads_per_block=heads_per_block,
                          transpose_out=transpose_out),
        grid=(m // tm, HEADS // heads_per_block),
        in_specs=[pl.BlockSpec((tm, tw), lambda i, j: (i, col0 * HEAD_W // tw + j)),
                  tab_spec, tab_spec, tab_spec,
                  pl.BlockSpec((1, HEAD_W), lambda i, j: (0, 0)),
                  pl.BlockSpec((HEAD_W, HEAD_W), lambda i, j: (0, 0))],
        out_specs=out_spec,
        out_shape=out_shape,
        compiler_params=_params(("parallel", "parallel"), tm * tw * 4 + 3 * tm * HEAD_W * 4,
                                tm * HEAD_W * 48),
        name="k_prep" if transpose_out else "q_prep",
    )(proj, *tables, gain.reshape(1, HEAD_W), group_ones)


def _rope_tables(pos):
    half = ROPE_DIMS // 2
    inv_freq = ROPE_THETA ** (-(jnp.arange(half, dtype=F32) * 2.0) / ROPE_DIMS)
    ang = pos.astype(F32)[:, None] * inv_freq[None, :]
    cos, sin = jnp.cos(ang), jnp.sin(ang)
    n = pos.shape[0]
    pad = QK_DIM - ROPE_DIMS
    ones, zeros, zh = jnp.ones((n, pad), F32), jnp.zeros((n, pad), F32), jnp.zeros((n, half), F32)
    cos_t = jnp.concatenate([cos, cos, ones], axis=1)
    sin_lo = jnp.concatenate([-sin, zh, zeros], axis=1)
    sin_hi = jnp.concatenate([zh, sin, zeros], axis=1)
    return tuple(jnp.tile(t, (1, 2)) for t in (cos_t, sin_lo, sin_hi))


def _nt_dot(a, b):
    return lax.dot_general(a, b, (((1,), (1,)), ((), ())), preferred_element_type=F32)


def _lane_repeat(x, n):
    return jnp.concatenate([x] * n, axis=1)


def _attn_kernel(q_ref, kt_ref, v_ref, ktm_ref, vm_ref, lam_ref, subln_ref, o_ref,
                 vp_ref, vpm_ref, *, tq, nq):
    ones = jnp.ones((vp_ref.shape[0], HEAD_W), BF16)
    vp_ref[:, :HEAD_W] = v_ref[...]
    vp_ref[:, HEAD_W:] = ones
    vpm_ref[:, :HEAD_W] = vm_ref[...]
    vpm_ref[:, HEAD_W:] = ones[:META_ROWS]

    lp = lam_ref[...]
    lam = (jnp.exp(jnp.sum(lp[0:1] * lp[1:2], axis=-1, keepdims=True))
           - jnp.exp(jnp.sum(lp[2:3] * lp[3:4], axis=-1, keepdims=True)) + LAM_INIT)
    lane = lax.broadcasted_iota(jnp.int32, (tq, HEAD_W), 1)
    col_m = lax.broadcasted_iota(jnp.int32, (tq, META_ROWS), 1)
    row = lax.broadcasted_iota(jnp.int32, (tq, tq), 0)
    col = lax.broadcasted_iota(jnp.int32, (tq, tq), 1)

    def q_maps(qv):
        q = q_ref[qv * tq:(qv + 1) * tq, :]
        zero = jnp.zeros_like(q)
        return (jnp.where(lane < QK_DIM, q, zero),
                jnp.where(lane >= QK_DIM, q, zero))

    def scores(task):
        qv, j = task
        kt = ktm_ref[0, 0, 0] if j is None else kt_ref[0, 0, j]
        return [jnp.dot(qc, kt, preferred_element_type=F32) for qc in q_maps(qv)]

    def consume(state, s_maps, vp, mask):
        out = []
        for (m_old, acc), s in zip(state, s_maps):
            if mask is not None:
                s = jnp.where(mask, s, NEG_INF)
            m_new = jnp.maximum(m_old, jnp.max(s, axis=-1, keepdims=True))
            p = jnp.exp2((s - _lane_repeat(m_new, s.shape[1] // HEAD_W)).astype(BF16))
            alpha = jnp.exp2(m_old - m_new)
            acc = (_lane_repeat(alpha, 2) * acc
                   + jnp.dot(p, vp, preferred_element_type=F32))
            out.append((m_new, acc))
        return out

    def finalize(qv, state):
        a1, a2 = state[0][1], state[1][1]
        o = (a1[:, :HEAD_W] / a1[:, HEAD_W:HEAD_W + 1]
             - lam * (a2[:, :HEAD_W] / a2[:, HEAD_W:HEAD_W + 1]))
        r = lax.rsqrt(jnp.mean(o * o, axis=-1, keepdims=True) + EPS)
        o_ref[qv * tq:(qv + 1) * tq, :] = (
            ((o * r) * subln_ref[...]) * (1.0 - LAM_INIT)).astype(o_ref.dtype)

    tasks = [(qv, j) for qv in range(nq) for j in [None] + list(range(qv + 1))]
    s_next = scores(tasks[0])
    state = None
    for t, (qv, j) in enumerate(tasks):
        s_cur = s_next
        if t + 1 < len(tasks):
            s_next = scores(tasks[t + 1])
        if j is None:
            state = [(jnp.full((tq, HEAD_W), NEG_INF, F32), jnp.zeros((tq, 2 * HEAD_W), F32))] * 2
            state = consume(state, s_cur, vpm_ref[...], col_m >= META_ROWS - N_META)
        else:
            state = consume(state, s_cur, vp_ref[j * tq:(j + 1) * tq, :],
                            col <= row if j == qv else None)
            if j == qv:
                finalize(qv, state)


def _attention(q, kt, proj, kt_meta, proj_meta, lam_params, subln, batch, seq, tq):
    nq = seq // tq
    kernel = functools.partial(_attn_kernel, tq=tq, nq=nq)
    blk = 4 * seq * HEAD_W * 2 + 2 * META_ROWS * HEAD_W * 2
    scratch_bytes = (seq + META_ROWS) * 2 * HEAD_W * 2
    return pl.pallas_call(
        kernel,
        grid=(batch, HEADS),
        in_specs=[pl.BlockSpec((seq, HEAD_W), lambda b, h: (b, h)),
                  pl.BlockSpec((1, 1, nq, HEAD_W, tq), lambda b, h: (b, h, 0, 0, 0)),
                  pl.BlockSpec((seq, HEAD_W), lambda b, h: (b, COL_DA_V + h)),
                  pl.BlockSpec((1, 1, 1, HEAD_W, META_ROWS), lambda b, h: (0, h, 0, 0, 0)),
                  pl.BlockSpec((META_ROWS, HEAD_W), lambda b, h: (0, META_COL_V + h)),
                  pl.BlockSpec((4, QK_DIM), lambda b, h: (0, 0)),
                  pl.BlockSpec((1, HEAD_W), lambda b, h: (0, 0))],
        out_specs=pl.BlockSpec((seq, HEAD_W), lambda b, h: (b, h)),
        out_shape=jax.ShapeDtypeStruct((batch * seq, HEADS * HEAD_W), BF16),
        scratch_shapes=[pltpu.VMEM((seq, 2 * HEAD_W), BF16),
                        pltpu.VMEM((META_ROWS, 2 * HEAD_W), BF16)],
        compiler_params=_params(("parallel", "parallel"), blk, scratch_bytes + 16 * tq * tq * 4),
        name="diff_attention",
    )(q, kt, proj, kt_meta, proj_meta, lam_params, subln)


def _lower_bound(lb_ref):
    x = lb_ref[...]
    e = jnp.exp(x - jnp.max(x, axis=0, keepdims=True))
    return e[0:1] / jnp.sum(e, axis=0, keepdims=True)


def _chunk_tri():
    r = jnp.arange(HG_CHUNK)
    return (r[None, :] <= r[:, None]).astype(BF16)


def _chunk_cumsum(tri, g):
    n = g.shape[0]
    parts = jnp.concatenate(_split_terms(g, 2), axis=2)
    s = lax.dot_general(jnp.broadcast_to(tri, (n,) + tri.shape), parts,
                        (((2,), (1,)), ((0,), (0,))), preferred_element_type=F32)
    return s[:, :, :HEAD_W] + s[:, :, HEAD_W:]


def _hg_block(q, fl, v, lb, tri, st, valid=None):
    c = HG_CHUNK
    n = q.shape[0] // c
    f = lb + (1.0 - lb) * jax.nn.sigmoid(fl)
    g = jnp.log(f)
    kk = 1.0 - f
    if valid is not None:
        g = jnp.where(valid, g, 0.0)
        kk = jnp.where(valid, kk, 0.0)
        v = jnp.where(valid, v, jnp.zeros_like(v))
    b = _chunk_cumsum(tri, g.reshape(n, c, HEAD_W))
    q3, k3, v3 = q.reshape(n, c, HEAD_W), kk.reshape(n, c, HEAD_W), v.reshape(n, c, HEAD_W)
    b_mid = b[:, c // 2 - 1:c // 2, :]
    b_last = b[:, c - 1:c, :]
    q_in = (q3 * jnp.exp(b - b_mid)).astype(BF16)
    k_in = (k3 * jnp.exp(b_mid - b)).astype(BF16)
    k_dec = (k3 * jnp.exp(b_last - b)).astype(BF16)
    q_ex = (q3 * jnp.exp(b)).astype(BF16)
    decay = jnp.exp(b_last)
    row = lax.broadcasted_iota(jnp.int32, (n, c, c), 1)
    col = lax.broadcasted_iota(jnp.int32, (n, c, c), 2)
    a = lax.dot_general(q_in, k_in, (((2,), (2,)), ((0,), (0,))), preferred_element_type=F32)
    a = jnp.where(col <= row, a, 0.0).astype(BF16)
    o_intra = lax.dot_general(a, v3, (((2,), (1,)), ((0,), (0,))), preferred_element_type=F32)
    u_t = [lax.dot_general(v3[i], k_dec[i], (((0,), (0,)), ((), ())), preferred_element_type=F32)
           for i in range(n)]
    outs = []
    for i in range(n):
        outs.append(o_intra[i] + _nt_dot(q_ex[i], st.astype(BF16)))
        st = st * decay[i] + u_t[i]
    return outs, st


def _hg_meta_kernel(f_ref, i_ref, tri_ref, lb_ref, s_ref):
    row = lax.broadcasted_iota(jnp.int32, (HG_CHUNK, HEAD_W), 0)
    _, st = _hg_block(jnp.zeros((HG_CHUNK, HEAD_W), F32), f_ref[...].astype(F32), i_ref[...],
                      _lower_bound(lb_ref), tri_ref[...], jnp.zeros((HEAD_W, HEAD_W), F32),
                      valid=row >= HG_CHUNK - N_META)
    s_ref[0] = st


def _hg_meta_state(proj_meta, lower_bound):
    blk = META_ROWS // HG_CHUNK - 1
    return pl.pallas_call(
        _hg_meta_kernel,
        grid=(HEADS,),
        in_specs=[pl.BlockSpec((HG_CHUNK, HEAD_W), lambda h: (blk, META_COL_F + h)),
                  pl.BlockSpec((HG_CHUNK, HEAD_W), lambda h: (blk, META_COL_I + h)),
                  pl.BlockSpec((HG_CHUNK, HG_CHUNK), lambda h: (0, 0)),
                  pl.BlockSpec((lower_bound.shape[0], HEAD_W), lambda h: (0, h))],
        out_specs=pl.BlockSpec((1, HEAD_W, HEAD_W), lambda h: (h, 0, 0)),
        out_shape=jax.ShapeDtypeStruct((HEADS, HEAD_W, HEAD_W), F32),
        compiler_params=_params(("parallel",), 1 << 20),
        name="hgrn2_meta_state",
    )(proj_meta, proj_meta, _chunk_tri(), lower_bound)


def _hgrn_kernel(q_ref, f_ref, i_ref, g_ref, tri_ref, lb_ref, s0_ref, on_ref, o_ref, st_ref):
    @pl.when(pl.program_id(2) == 0)
    def _():
        st_ref[...] = s0_ref[0]

    outs, st = _hg_block(q_ref[...].astype(F32), f_ref[...].astype(F32), i_ref[...],
                         _lower_bound(lb_ref), tri_ref[...], st_ref[...])
    st_ref[...] = st
    for i, o in enumerate(outs):
        sl = pl.ds(i * HG_CHUNK, HG_CHUNK)
        r = lax.rsqrt(jnp.mean(o * o, axis=-1, keepdims=True) + EPS)
        y = ((o * r) * on_ref[...]) * jax.nn.sigmoid(g_ref[sl, :].astype(F32))
        o_ref[sl, :] = y.astype(o_ref.dtype)


def _hgrn2(proj, s0, lower_bound, out_norm, batch, seq, tb):
    nt = seq // tb

    def col(off):
        return pl.BlockSpec((tb, HEAD_W), lambda b, h, t: (b * nt + t, off + h))

    return pl.pallas_call(
        _hgrn_kernel,
        grid=(batch, HEADS, nt),
        in_specs=[col(COL_HG_Q), col(COL_HG_F), col(COL_HG_I), col(COL_HG_G),
                  pl.BlockSpec((HG_CHUNK, HG_CHUNK), lambda b, h, t: (0, 0)),
                  pl.BlockSpec((lower_bound.shape[0], HEAD_W), lambda b, h, t: (0, h)),
                  pl.BlockSpec((1, HEAD_W, HEAD_W), lambda b, h, t: (h, 0, 0)),
                  pl.BlockSpec((1, HEAD_W), lambda b, h, t: (0, 0))],
        out_specs=pl.BlockSpec((tb, HEAD_W), lambda b, h, t: (b * nt + t, h)),
        out_shape=jax.ShapeDtypeStruct((batch * seq, HEADS * HEAD_W), BF16),
        scratch_shapes=[pltpu.VMEM((HEAD_W, HEAD_W), F32)],
        compiler_params=_params(("parallel", "parallel", "arbitrary"),
                                5 * tb * HEAD_W * 2, 24 * tb * HEAD_W * 4),
        name="hgrn2",
    )(proj, proj, proj, proj, _chunk_tri(), lower_bound, s0, out_norm)


def _merge_kernel(ya_ref, yb_ref, wa_ref, wb_ref, ga_ref, gb_ref, o_ref, wa_bf_ref, wb_bf_ref):
    @pl.when(pl.program_id(1) == 0)
    def _():
        wa_bf_ref[...] = wa_ref[...].astype(BF16)
        wb_bf_ref[...] = wb_ref[...].astype(BF16)

    a = jnp.dot(ya_ref[...], wa_bf_ref[...], preferred_element_type=F32)
    b = jnp.dot(yb_ref[...], wb_bf_ref[...], preferred_element_type=F32)
    o = (jax.nn.sigmoid(ga_ref[...].astype(F32)) * a + jax.nn.sigmoid(gb_ref[...].astype(F32)) * b)
    o_ref[...] = o.astype(o_ref.dtype)


def _gated_merge(y_a, y_b, w_a, w_b, proj, tm, tn):
    m, kdim = y_a.shape
    n = w_a.shape[1]
    tm = min(tm, m)
    blk = 2 * tm * kdim * 2 + 2 * kdim * tn * 4 + 3 * tm * tn * 2
    return pl.pallas_call(
        _merge_kernel,
        grid=(n // tn, m // tm),
        in_specs=[pl.BlockSpec((tm, kdim), lambda j, i: (i, 0)),
                  pl.BlockSpec((tm, kdim), lambda j, i: (i, 0)),
                  pl.BlockSpec((kdim, tn), lambda j, i: (0, j)),
                  pl.BlockSpec((kdim, tn), lambda j, i: (0, j)),
                  pl.BlockSpec((tm, tn), lambda j, i: (i, COL_GATE_A // tn + j)),
                  pl.BlockSpec((tm, tn), lambda j, i: (i, COL_GATE_B // tn + j))],
        out_specs=pl.BlockSpec((tm, tn), lambda j, i: (i, j)),
        out_shape=jax.ShapeDtypeStruct((m, n), BF16),
        scratch_shapes=[pltpu.VMEM((kdim, tn), BF16), pltpu.VMEM((kdim, tn), BF16)],
        compiler_params=_params(("arbitrary", "arbitrary"), blk,
                                2 * kdim * tn * 2 + 4 * tm * tn * 4),
        name="gated_merge",
    )(y_a, y_b, w_a, w_b, proj, proj)


def kernel(x, meta_tokens, norm_mix, w_in, da_q_norm, da_k_norm, da_lambda_q1, da_lambda_k1,
           da_lambda_q2, da_lambda_k2, da_subln, hg_lower_bound, hg_out_norm, w_up_a, w_up_b,
           w_out, norm_mlp, w_ff1, w_ff2):
    batch, seq, d = x.shape
    assert d == D_MODEL and w_in.shape[0] == 1 and w_in.shape[2] == IN_WIDTH
    assert meta_tokens.shape == (N_META, D_MODEL) and seq % 512 == 0
    m = batch * seq
    xf = x.reshape(m, d)
    meta_pad = jnp.concatenate([jnp.zeros((META_ROWS - N_META, d), F32), meta_tokens.astype(F32)])

    tn = 512
    u = _rmsnorm(xf, norm_mix[0], 256)
    u_meta = _rmsnorm(meta_pad, norm_mix[0], META_ROWS)
    proj = _matmul_stream(u, w_in[0], tm=1024, tn=1024, out_dtype=BF16, name="in_proj")
    kv_blocks = 2 * HEADS * HEAD_W // tn

    def meta_col(j):
        return jnp.where(j < kv_blocks, COL_DA_K * HEAD_W // tn + j,
                         COL_HG_F * HEAD_W // tn + j - kv_blocks)

    proj_meta = _matmul_wcast(u_meta, w_in[0], tm=META_ROWS, tn=tn, out_dtype=BF16,
                              name="in_proj_meta", w_cols=(META_WIDTH, meta_col))

    q_gain = jnp.tile(da_q_norm[0], 2) * (QK_DIM ** -0.5 * math.log2(math.e))
    k_gain = jnp.tile(da_k_norm[0], 2)
    pos_meta = jnp.maximum(jnp.arange(META_ROWS, dtype=jnp.int32) - (META_ROWS - N_META), 0)
    tab_main = _rope_tables(jnp.arange(seq, dtype=jnp.int32) + N_META)
    q_hat = _qk_prep(proj, COL_DA_Q, tab_main, q_gain, ATT_BLOCK, seq, 4, False)
    kt = _qk_prep(proj, COL_DA_K, tab_main, k_gain, ATT_BLOCK, seq, 4, True)
    kt_meta = _qk_prep(proj_meta, META_COL_K, _rope_tables(pos_meta), k_gain, META_ROWS, META_ROWS,
                       4, True)
    lam_params = jnp.stack([da_lambda_q1[0], da_lambda_k1[0], da_lambda_q2[0], da_lambda_k2[0]])
    y_a = _attention(q_hat, kt, proj, kt_meta, proj_meta, lam_params,
                     da_subln[0].reshape(1, HEAD_W), batch, seq, ATT_BLOCK)

    s0 = _hg_meta_state(proj_meta, hg_lower_bound)
    y_b = _hgrn2(proj, s0, hg_lower_bound, hg_out_norm[0].reshape(1, HEAD_W), batch, seq, 2048)

    merged = _gated_merge(y_a, y_b, w_up_a[0], w_up_b[0], proj, 1024, tn)
    h1 = _matmul_stream(merged, w_out[0], tm=1024, tn=512, out_dtype=F32, res=xf,
                        name="out_proj")

    v = _rmsnorm(h1, norm_mlp[0], 256)
    hid, w_ff2_b = _matmul_stream(v, w_ff1[0], tm=1024, tn=1024, out_dtype=BF16, relu2=True,
                                  side=w_ff2[0], name="ff1")
    out = _matmul_ktiled(hid, w_ff2_b, h1, tm=1024, tn=1024, tk=4096, name="ff2")
    return out.reshape(batch, seq, d)
```

```python
import functools
import math

import jax
import jax.numpy as jnp
from jax import lax
from jax.experimental import pallas as pl
from jax.experimental.pallas import tpu as pltpu

F32 = jnp.float32
BF16 = jnp.bfloat16

D_MODEL = 4096
N_META = 16
HEADS = 16
HEAD_W = 128
QK_DIM = 64
ROPE_DIMS = 16
ROPE_THETA = 500000.0
HG_CHUNK = 64
META_ROWS = 128
ATT_BLOCK = 512
EPS = 1e-6
NEG_INF = -1e30
LAM_INIT = 0.8 - 0.6 * math.exp(-0.3 * 0)

COL_DA_Q, COL_DA_K, COL_DA_V = 0, 16, 32
COL_HG_Q, COL_HG_F, COL_HG_I, COL_HG_G = 48, 64, 80, 96
COL_GATE_A, COL_GATE_B = 112 * HEAD_W, 144 * HEAD_W
IN_WIDTH = 176 * HEAD_W

V7X_VMEM_LIMIT_CAP = 58 * 2**20


def _params(semantics, block_bytes, extra_bytes=0):
    need = 2 * block_bytes + extra_bytes + (4 << 20)
    return pltpu.CompilerParams(dimension_semantics=semantics,
                                vmem_limit_bytes=min(max(need, 16 << 20), V7X_VMEM_LIMIT_CAP))


def _sigmoid(x):
    return 0.5 * jnp.tanh(0.5 * x) + 0.5


def _rmsnorm_kernel(x_ref, g_ref, o_ref):
    x = x_ref[...]
    r = lax.rsqrt(jnp.mean(x * x, axis=-1, keepdims=True) + EPS)
    o_ref[...] = ((x * r) * g_ref[...]).astype(o_ref.dtype)


def _rmsnorm(x, g, tm):
    m, d = x.shape
    return pl.pallas_call(
        _rmsnorm_kernel,
        grid=(m // tm,),
        in_specs=[pl.BlockSpec((tm, d), lambda i: (i, 0)),
                  pl.BlockSpec((1, d), lambda i: (0, 0))],
        out_specs=pl.BlockSpec((tm, d), lambda i: (i, 0)),
        out_shape=jax.ShapeDtypeStruct((m, d), BF16),
        compiler_params=_params(("parallel",), tm * d * 6),
        name="rmsnorm",
    )(x, g.reshape(1, d))


def _mm_ktiled_kernel(a_ref, w_ref, res_ref, o_ref):
    part = jnp.dot(a_ref[...], w_ref[...], preferred_element_type=F32)
    k = pl.program_id(2)

    @pl.when(k == 0)
    def _():
        o_ref[...] = part + res_ref[...]

    @pl.when(k > 0)
    def _():
        o_ref[...] += part


def _matmul_ktiled(a, w, res, *, tm, tn, tk, name):
    m, kdim = a.shape
    n = w.shape[1]
    tm = min(tm, m)
    block_bytes = tm * tk * 2 + tk * tn * 2 + 2 * tm * tn * 4
    return pl.pallas_call(
        _mm_ktiled_kernel,
        grid=(m // tm, n // tn, kdim // tk),
        in_specs=[pl.BlockSpec((tm, tk), lambda i, j, k: (i, k)),
                  pl.BlockSpec((tk, tn), lambda i, j, k: (k, j)),
                  pl.BlockSpec((tm, tn), lambda i, j, k: (i, j))],
        out_specs=pl.BlockSpec((tm, tn), lambda i, j, k: (i, j)),
        out_shape=jax.ShapeDtypeStruct((m, n), F32),
        compiler_params=_params(("parallel", "parallel", "arbitrary"), block_bytes,
                                2 * tm * tn * 4),
        name=name,
    )(a, w, res)


def _mm_stream_kernel(*refs, relu2, has_res, has_side, extra_cols, n_col, n_row):
    refs = list(refs)
    a_ref, w_hbm = refs.pop(0), refs.pop(0)
    res_ref = refs.pop(0) if has_res else None
    side_hbm = refs.pop(0) if has_side else None
    extra_a_ref = refs.pop(0) if extra_cols else None
    o_ref = refs.pop(0)
    side_out_hbm = refs.pop(0) if has_side else None
    extra_o_ref = refs.pop(0) if extra_cols else None
    wb_ref, stage_ref, sem = refs.pop(0), refs.pop(0), refs.pop(0)
    j, i = pl.program_id(0), pl.program_id(1)
    ck, tn = stage_ref.shape[1], stage_ref.shape[2]

    def chunk_copy(col, c, slot):
        return pltpu.make_async_copy(
            w_hbm.at[pl.ds(pl.multiple_of(c * ck, ck), ck), pl.ds(pl.multiple_of(col * tn, tn), tn)],
            stage_ref.at[slot], sem.at[slot])

    def lookahead_col(jj):
        return jnp.minimum(jj + 1, n_col - 1)

    @pl.when((j == 0) & (i == 0))
    def _():
        for c in range(n_row):
            cp = chunk_copy(0, c, c % 2)
            cp.start()
            cp.wait()
            wb_ref[0, c * ck:(c + 1) * ck, :] = stage_ref[c % 2].astype(BF16)
        chunk_copy(lookahead_col(0), 0, 0).start()

    acc = jnp.dot(a_ref[...], wb_ref[j % 2], preferred_element_type=F32)
    if relu2:
        acc = jnp.square(jnp.maximum(acc, 0.0))
    if has_res:
        acc = acc + res_ref[...]
    o_ref[...] = acc.astype(o_ref.dtype)

    chunk_copy(lookahead_col(j), i, i % 2).wait()
    wb_ref[(j + 1) % 2, pl.ds(pl.multiple_of(i * ck, ck), ck), :] = stage_ref[i % 2].astype(BF16)

    @pl.when((j < n_col - 1) | (i < n_row - 1))
    def _():
        wrap = i == n_row - 1
        chunk_copy(lookahead_col(jnp.where(wrap, j + 1, j)), jnp.where(wrap, 0, i + 1),
                   (i + 1) % 2).start()

    if extra_cols:
        wanted = functools.reduce(jnp.logical_or,
                                  [(j >= lo) & (j < lo + cnt) for lo, cnt in extra_cols])

        @pl.when((i == 0) & wanted)
        def _():
            extra_o_ref[...] = jnp.dot(extra_a_ref[...], wb_ref[j % 2],
                                       preferred_element_type=F32).astype(extra_o_ref.dtype)

        @pl.when((i == 0) & jnp.logical_not(wanted))
        def _():
            extra_o_ref[...] = jnp.zeros_like(extra_o_ref)

    if has_side:
        in_buf, out_buf, in_sem, out_sem = refs
        rs = in_buf.shape[1]
        t, n_steps = j * n_row + i, n_col * n_row

        def side_in(tt):
            return pltpu.make_async_copy(side_hbm.at[pl.ds(pl.multiple_of(tt * rs, rs), rs), :],
                                         in_buf.at[tt % 2], in_sem.at[tt % 2])

        def side_out(tt):
            return pltpu.make_async_copy(out_buf.at[tt % 2],
                                         side_out_hbm.at[pl.ds(pl.multiple_of(tt * rs, rs), rs), :],
                                         out_sem.at[tt % 2])

        @pl.when(t == 0)
        def _():
            side_in(t).start()

        @pl.when(t + 1 < n_steps)
        def _():
            side_in(t + 1).start()

        side_in(t).wait()

        @pl.when(t >= 2)
        def _():
            side_out(t - 2).wait()

        out_buf[t % 2] = in_buf[t % 2].astype(BF16)
        side_out(t).start()

        @pl.when(t == n_steps - 1)
        def _():
            side_out(t - 1).wait()
            side_out(t).wait()


def _matmul_stream(a, w, *, tm, tn, out_dtype, relu2=False, res=None, side=None, extra=None, name):
    m, kdim = a.shape
    n = w.shape[1]
    n_row, n_col = m // tm, n // tn
    assert n_row % 2 == 0 and n_col >= 2 and kdim % n_row == 0
    ck = kdim // n_row
    in_specs = [pl.BlockSpec((tm, kdim), lambda j, i: (i, 0)),
                pl.BlockSpec(memory_space=pl.ANY)]
    args = [a, w]
    out_bytes = jnp.dtype(out_dtype).itemsize
    block_bytes = tm * kdim * 2 + tm * tn * out_bytes
    if res is not None:
        in_specs.append(pl.BlockSpec((tm, tn), lambda j, i: (i, j)))
        args.append(res)
        block_bytes += tm * tn * 4
    out_specs = [pl.BlockSpec((tm, tn), lambda j, i: (i, j))]
    out_shape = [jax.ShapeDtypeStruct((m, n), out_dtype)]
    scratch = [pltpu.VMEM((2, kdim, tn), BF16), pltpu.VMEM((2, ck, tn), F32),
               pltpu.SemaphoreType.DMA((2,))]
    scratch_bytes = 2 * kdim * tn * 2 + 2 * ck * tn * 4
    if side is not None:
        rows, cols = side.shape
        assert rows % (n_row * n_col) == 0
        rs = rows // (n_row * n_col)
        in_specs.append(pl.BlockSpec(memory_space=pl.ANY))
        args.append(side)
        out_specs.append(pl.BlockSpec(memory_space=pl.ANY))
        out_shape.append(jax.ShapeDtypeStruct(side.shape, BF16))
        scratch += [pltpu.VMEM((2, rs, cols), F32), pltpu.VMEM((2, rs, cols), BF16),
                    pltpu.SemaphoreType.DMA((2,)), pltpu.SemaphoreType.DMA((2,))]
        scratch_bytes += 2 * rs * cols * 6
    extra_cols = None
    if extra is not None:
        extra_rows, extra_cols = extra
        r = extra_rows.shape[0]
        in_specs.append(pl.BlockSpec((r, kdim), lambda j, i: (0, 0)))
        args.append(extra_rows)
        out_specs.append(pl.BlockSpec((r, tn), lambda j, i: (0, j)))
        out_shape.append(jax.ShapeDtypeStruct((r, n), BF16))
        block_bytes += r * kdim * 2 + r * tn * 2
    results = pl.pallas_call(
        functools.partial(_mm_stream_kernel, relu2=relu2, has_res=res is not None,
                          has_side=side is not None, extra_cols=extra_cols,
                          n_col=n_col, n_row=n_row),
        grid=(n_col, n_row),
        in_specs=in_specs,
        out_specs=out_specs,
        out_shape=out_shape,
        scratch_shapes=scratch,
        compiler_params=_params(("arbitrary", "arbitrary"), block_bytes,
                                scratch_bytes + tm * tn * 4),
        name=name,
    )(*args)
    return results[0] if len(results) == 1 else tuple(results)


def _split_terms(x, n):
    terms = []
    for _ in range(n - 1):
        t = x.astype(BF16)
        terms.append(t)
        x = x - t.astype(F32)
    terms.append(x.astype(BF16))
    return terms


def _qk_prep_kernel(x_ref, cos_ref, sin_ref, g_ref, ones_ref, swap_ref, o_ref, *,
                    heads_per_block, transpose_out):
    g, cos, sin = g_ref[...], cos_ref[...], sin_ref[...]
    for hh in range(heads_per_block):
        sl = slice(hh * HEAD_W, (hh + 1) * HEAD_W)
        x = x_ref[:, sl].astype(F32)
        ss = jnp.dot((x * x).astype(BF16), ones_ref[...], preferred_element_type=F32)
        y = (x * lax.rsqrt(ss * (1.0 / QK_DIM) + EPS)) * g
        partner = jnp.dot(jnp.concatenate(_split_terms(y, 2), axis=1), swap_ref[...],
                          preferred_element_type=F32)
        y = y * cos + partner * sin
        if transpose_out:
            o_ref[0, hh, 0] = y.T.astype(o_ref.dtype)
        else:
            o_ref[:, sl] = y.astype(o_ref.dtype)


def _qk_prep(proj, col0, tables, gain, tm, rows_per_seq, heads_per_block, transpose_out):
    m = proj.shape[0]
    nblk = rows_per_seq // tm
    tw = heads_per_block * HEAD_W
    lane_group = jnp.arange(HEAD_W) // QK_DIM
    group_ones = (lane_group[:, None] == lane_group[None, :]).astype(BF16)
    tab_spec = pl.BlockSpec((tm, HEAD_W), lambda i, j: (i % nblk, 0))
    lanes = jnp.arange(HEAD_W)
    in_rope = lanes % QK_DIM < ROPE_DIMS
    first_half = lanes % QK_DIM < ROPE_DIMS // 2
    partner_lane = jnp.where(first_half, lanes + ROPE_DIMS // 2, lanes - ROPE_DIMS // 2)
    swap = jnp.where(in_rope[None, :] & (lanes[:, None] == partner_lane[None, :]),
                     jnp.where(first_half, -1.0, 1.0)[None, :], 0.0).astype(BF16)
    swap = jnp.tile(swap, (2, 1))
    if transpose_out:
        out_spec = pl.BlockSpec((1, heads_per_block, 1, HEAD_W, tm),
                                lambda i, j: (i // nblk, j, i % nblk, 0, 0))
        out_shape = jax.ShapeDtypeStruct((m // rows_per_seq, HEADS, nblk, HEAD_W, tm), BF16)
    else:
        out_spec = pl.BlockSpec((tm, tw), lambda i, j: (i, j))
        out_shape = jax.ShapeDtypeStruct((m, HEADS * HEAD_W), BF16)
    return pl.pallas_call(
        functools.partial(_qk_prep_kernel, heads_per_block=heads_per_block,
                          transpose_out=transpose_out),
        grid=(m // tm, HEADS // heads_per_block),
        in_specs=[pl.BlockSpec((tm, tw), lambda i, j: (i, col0 * HEAD_W // tw + j)),
                  tab_spec, tab_spec,
                  pl.BlockSpec((1, HEAD_W), lambda i, j: (0, 0)),
                  pl.BlockSpec((HEAD_W, HEAD_W), lambda i, j: (0, 0)),
                  pl.BlockSpec((2 * HEAD_W, HEAD_W), lambda i, j: (0, 0))],
        out_specs=out_spec,
        out_shape=out_shape,
        compiler_params=_params(("parallel", "parallel"), tm * tw * 4 + 3 * tm * HEAD_W * 4,
                                tm * HEAD_W * 48),
        name="k_prep" if transpose_out else "q_prep",
    )(proj, *tables, gain.reshape(1, HEAD_W), group_ones, swap)


def _rope_tables(pos):
    half = ROPE_DIMS // 2
    inv_freq = ROPE_THETA ** (-(jnp.arange(half, dtype=F32) * 2.0) / ROPE_DIMS)
    ang = pos.astype(F32)[:, None] * inv_freq[None, :]
    cos, sin = jnp.cos(ang), jnp.sin(ang)
    n = pos.shape[0]
    pad = QK_DIM - ROPE_DIMS
    cos_t = jnp.concatenate([cos, cos, jnp.ones((n, pad), F32)], axis=1)
    sin_t = jnp.concatenate([sin, sin, jnp.zeros((n, pad), F32)], axis=1)
    return jnp.tile(cos_t, (1, 2)), jnp.tile(sin_t, (1, 2))


def _nt_dot(a, b):
    return lax.dot_general(a, b, (((1,), (1,)), ((), ())), preferred_element_type=F32)


def _lane_repeat(x, n):
    return jnp.concatenate([x] * n, axis=1)


def _attn_kernel(q_ref, kt_ref, v_ref, ktm_ref, vm_ref, lam_ref, subln_ref, o_ref,
                 vp_ref, vpm_ref, *, tq, nq):
    ones = jnp.ones((vp_ref.shape[0], HEAD_W), BF16)
    vp_ref[:, :HEAD_W] = v_ref[...]
    vp_ref[:, HEAD_W:] = ones
    vpm_ref[:, :HEAD_W] = vm_ref[...]
    vpm_ref[:, HEAD_W:] = ones[:META_ROWS]

    lp = lam_ref[...]
    lam = (jnp.exp(jnp.sum(lp[0:1] * lp[1:2], axis=-1, keepdims=True))
           - jnp.exp(jnp.sum(lp[2:3] * lp[3:4], axis=-1, keepdims=True)) + LAM_INIT)
    lane = lax.broadcasted_iota(jnp.int32, (tq, HEAD_W), 1)
    col_m = lax.broadcasted_iota(jnp.int32, (tq, META_ROWS), 1)
    row = lax.broadcasted_iota(jnp.int32, (tq, tq), 0)
    col = lax.broadcasted_iota(jnp.int32, (tq, tq), 1)

    def q_maps(qv):
        q = q_ref[qv * tq:(qv + 1) * tq, :]
        zero = jnp.zeros_like(q)
        return (jnp.where(lane < QK_DIM, q, zero),
                jnp.where(lane >= QK_DIM, q, zero))

    def scores(task):
        qv, j = task
        kt = ktm_ref[0, 0, 0] if j is None else kt_ref[0, 0, j]
        return [jnp.dot(qc, kt, preferred_element_type=F32) for qc in q_maps(qv)]

    def consume(state, s_maps, vp, mask):
        out = []
        for (m_old, acc), s in zip(state, s_maps):
            if mask is not None:
                s = jnp.where(mask, s, NEG_INF)
            m_new = jnp.maximum(m_old, jnp.max(s, axis=-1, keepdims=True))
            p = jnp.exp2((s - _lane_repeat(m_new, s.shape[1] // HEAD_W)).astype(BF16))
            alpha = jnp.exp2(m_old - m_new)
            acc = (_lane_repeat(alpha, 2) * acc
                   + jnp.dot(p, vp, preferred_element_type=F32))
            out.append((m_new, acc))
        return out

    def finalize(qv, state):
        a1, a2 = state[0][1], state[1][1]
        o = (a1[:, :HEAD_W] / a1[:, HEAD_W:HEAD_W + 1]
             - lam * (a2[:, :HEAD_W] / a2[:, HEAD_W:HEAD_W + 1]))
        r = lax.rsqrt(jnp.mean(o * o, axis=-1, keepdims=True) + EPS)
        o_ref[qv * tq:(qv + 1) * tq, :] = (
            ((o * r) * subln_ref[...]) * (1.0 - LAM_INIT)).astype(o_ref.dtype)

    tasks = [(qv, j) for qv in range(nq) for j in [None] + list(range(qv + 1))]
    s_next = scores(tasks[0])
    state = None
    for t, (qv, j) in enumerate(tasks):
        s_cur = s_next
        if t + 1 < len(tasks):
            s_next = scores(tasks[t + 1])
        if j is None:
            state = [(jnp.full((tq, HEAD_W), NEG_INF, F32), jnp.zeros((tq, 2 * HEAD_W), F32))] * 2
            state = consume(state, s_cur, vpm_ref[...], col_m >= META_ROWS - N_META)
        else:
            state = consume(state, s_cur, vp_ref[j * tq:(j + 1) * tq, :],
                            col <= row if j == qv else None)
            if j == qv:
                finalize(qv, state)


def _attention(q, kt, proj, kt_meta, proj_meta, lam_params, subln, batch, seq, tq):
    nq = seq // tq
    kernel = functools.partial(_attn_kernel, tq=tq, nq=nq)
    blk = 4 * seq * HEAD_W * 2 + 2 * META_ROWS * HEAD_W * 2
    scratch_bytes = (seq + META_ROWS) * 2 * HEAD_W * 2
    return pl.pallas_call(
        kernel,
        grid=(batch, HEADS),
        in_specs=[pl.BlockSpec((seq, HEAD_W), lambda b, h: (b, h)),
                  pl.BlockSpec((1, 1, nq, HEAD_W, tq), lambda b, h: (b, h, 0, 0, 0)),
                  pl.BlockSpec((seq, HEAD_W), lambda b, h: (b, COL_DA_V + h)),
                  pl.BlockSpec((1, 1, 1, HEAD_W, META_ROWS), lambda b, h: (0, h, 0, 0, 0)),
                  pl.BlockSpec((META_ROWS, HEAD_W), lambda b, h: (0, COL_DA_V + h)),
                  pl.BlockSpec((4, QK_DIM), lambda b, h: (0, 0)),
                  pl.BlockSpec((1, HEAD_W), lambda b, h: (0, 0))],
        out_specs=pl.BlockSpec((seq, HEAD_W), lambda b, h: (b, h)),
        out_shape=jax.ShapeDtypeStruct((batch * seq, HEADS * HEAD_W), BF16),
        scratch_shapes=[pltpu.VMEM((seq, 2 * HEAD_W), BF16),
                        pltpu.VMEM((META_ROWS, 2 * HEAD_W), BF16)],
        compiler_params=_params(("parallel", "parallel"), blk, scratch_bytes + 16 * tq * tq * 4),
        name="diff_attention",
    )(q, kt, proj, kt_meta, proj_meta, lam_params, subln)


def _lower_bound(lb_ref):
    x = lb_ref[...]
    e = jnp.exp(x - jnp.max(x, axis=0, keepdims=True))
    return e[0:1] / jnp.sum(e, axis=0, keepdims=True)


def _chunk_tri():
    r = jnp.arange(HG_CHUNK)
    return (r[None, :] <= r[:, None]).astype(BF16)


def _chunk_cumsum(tri, g):
    n = g.shape[0]
    parts = jnp.concatenate(_split_terms(g, 2), axis=2)
    s = lax.dot_general(jnp.broadcast_to(tri, (n,) + tri.shape), parts,
                        (((2,), (1,)), ((0,), (0,))), preferred_element_type=F32)
    return s[:, :, :HEAD_W] + s[:, :, HEAD_W:]


def _hg_block(q, fl, v, lb, tri, st, valid=None):
    c = HG_CHUNK
    n = q.shape[0] // c
    f = lb + (1.0 - lb) * _sigmoid(fl)
    g = jnp.log(f)
    kk = 1.0 - f
    if valid is not None:
        g = jnp.where(valid, g, 0.0)
        kk = jnp.where(valid, kk, 0.0)
        v = jnp.where(valid, v, jnp.zeros_like(v))
    b = _chunk_cumsum(tri, g.reshape(n, c, HEAD_W))
    q3, k3, v3 = q.reshape(n, c, HEAD_W), kk.reshape(n, c, HEAD_W), v.reshape(n, c, HEAD_W)
    b_mid = b[:, c // 2 - 1:c // 2, :]
    b_last = b[:, c - 1:c, :]
    q_in = (q3 * jnp.exp(b - b_mid)).astype(BF16)
    k_in = (k3 * jnp.exp(b_mid - b)).astype(BF16)
    k_dec = (k3 * jnp.exp(b_last - b)).astype(BF16)
    q_ex = (q3 * jnp.exp(b)).astype(BF16)
    decay = jnp.exp(b_last)
    row = lax.broadcasted_iota(jnp.int32, (n, c, c), 1)
    col = lax.broadcasted_iota(jnp.int32, (n, c, c), 2)
    a = lax.dot_general(q_in, k_in, (((2,), (2,)), ((0,), (0,))), preferred_element_type=F32)
    a = jnp.where(col <= row, a, 0.0).astype(BF16)
    o_intra = lax.dot_general(a, v3, (((2,), (1,)), ((0,), (0,))), preferred_element_type=F32)
    u_t = [lax.dot_general(v3[i], k_dec[i], (((0,), (0,)), ((), ())), preferred_element_type=F32)
           for i in range(n)]
    outs = []
    for i in range(n):
        outs.append(o_intra[i] + _nt_dot(q_ex[i], st.astype(BF16)))
        st = st * decay[i] + u_t[i]
    return outs, st


def _hg_meta_kernel(f_ref, i_ref, tri_ref, lb_ref, s_ref):
    row = lax.broadcasted_iota(jnp.int32, (HG_CHUNK, HEAD_W), 0)
    _, st = _hg_block(jnp.zeros((HG_CHUNK, HEAD_W), F32), f_ref[...].astype(F32), i_ref[...],
                      _lower_bound(lb_ref), tri_ref[...], jnp.zeros((HEAD_W, HEAD_W), F32),
                      valid=row >= HG_CHUNK - N_META)
    s_ref[0] = st


def _hg_meta_state(proj_meta, lower_bound):
    blk = META_ROWS // HG_CHUNK - 1
    return pl.pallas_call(
        _hg_meta_kernel,
        grid=(HEADS,),
        in_specs=[pl.BlockSpec((HG_CHUNK, HEAD_W), lambda h: (blk, COL_HG_F + h)),
                  pl.BlockSpec((HG_CHUNK, HEAD_W), lambda h: (blk, COL_HG_I + h)),
                  pl.BlockSpec((HG_CHUNK, HG_CHUNK), lambda h: (0, 0)),
                  pl.BlockSpec((lower_bound.shape[0], HEAD_W), lambda h: (0, h))],
        out_specs=pl.BlockSpec((1, HEAD_W, HEAD_W), lambda h: (h, 0, 0)),
        out_shape=jax.ShapeDtypeStruct((HEADS, HEAD_W, HEAD_W), F32),
        compiler_params=_params(("parallel",), 1 << 20),
        name="hgrn2_meta_state",
    )(proj_meta, proj_meta, _chunk_tri(), lower_bound)


def _hgrn_kernel(q_ref, f_ref, i_ref, g_ref, tri_ref, lb_ref, s0_ref, on_ref, o_ref, st_ref):
    @pl.when(pl.program_id(2) == 0)
    def _():
        st_ref[...] = s0_ref[0]

    outs, st = _hg_block(q_ref[...].astype(F32), f_ref[...].astype(F32), i_ref[...],
                         _lower_bound(lb_ref), tri_ref[...], st_ref[...])
    st_ref[...] = st
    for i, o in enumerate(outs):
        sl = pl.ds(i * HG_CHUNK, HG_CHUNK)
        r = lax.rsqrt(jnp.mean(o * o, axis=-1, keepdims=True) + EPS)
        y = ((o * r) * on_ref[...]) * _sigmoid(g_ref[sl, :].astype(F32))
        o_ref[sl, :] = y.astype(o_ref.dtype)


def _hgrn2(proj, s0, lower_bound, out_norm, batch, seq, tb):
    nt = seq // tb

    def col(off):
        return pl.BlockSpec((tb, HEAD_W), lambda b, h, t: (b * nt + t, off + h))

    return pl.pallas_call(
        _hgrn_kernel,
        grid=(batch, HEADS, nt),
        in_specs=[col(COL_HG_Q), col(COL_HG_F), col(COL_HG_I), col(COL_HG_G),
                  pl.BlockSpec((HG_CHUNK, HG_CHUNK), lambda b, h, t: (0, 0)),
                  pl.BlockSpec((lower_bound.shape[0], HEAD_W), lambda b, h, t: (0, h)),
                  pl.BlockSpec((1, HEAD_W, HEAD_W), lambda b, h, t: (h, 0, 0)),
                  pl.BlockSpec((1, HEAD_W), lambda b, h, t: (0, 0))],
        out_specs=pl.BlockSpec((tb, HEAD_W), lambda b, h, t: (b * nt + t, h)),
        out_shape=jax.ShapeDtypeStruct((batch * seq, HEADS * HEAD_W), BF16),
        scratch_shapes=[pltpu.VMEM((HEAD_W, HEAD_W), F32)],
        compiler_params=_params(("parallel", "parallel", "arbitrary"),
                                5 * tb * HEAD_W * 2, 24 * tb * HEAD_W * 4),
        name="hgrn2",
    )(proj, proj, proj, proj, _chunk_tri(), lower_bound, s0, out_norm)


def _merge_kernel(ya_ref, yb_ref, wa_ref, wb_ref, ga_ref, gb_ref, o_ref, wa_bf_ref, wb_bf_ref):
    @pl.when(pl.program_id(1) == 0)
    def _():
        wa_bf_ref[...] = wa_ref[...].astype(BF16)
        wb_bf_ref[...] = wb_ref[...].astype(BF16)

    a = jnp.dot(ya_ref[...], wa_bf_ref[...], preferred_element_type=F32)
    b = jnp.dot(yb_ref[...], wb_bf_ref[...], preferred_element_type=F32)
    o = (_sigmoid(ga_ref[...].astype(F32)) * a + _sigmoid(gb_ref[...].astype(F32)) * b)
    o_ref[...] = o.astype(o_ref.dtype)


def _gated_merge(y_a, y_b, w_a, w_b, proj, tm, tn):
    m = y_a.shape[0]
    kdim, n = w_a.shape
    tm = min(tm, m)
    blk = 2 * tm * kdim * 2 + 2 * kdim * tn * 4 + 3 * tm * tn * 2
    return pl.pallas_call(
        _merge_kernel,
        grid=(n // tn, m // tm),
        in_specs=[pl.BlockSpec((tm, kdim), lambda j, i: (i, 0)),
                  pl.BlockSpec((tm, kdim), lambda j, i: (i, 0)),
                  pl.BlockSpec((kdim, tn), lambda j, i: (0, j)),
                  pl.BlockSpec((kdim, tn), lambda j, i: (0, j)),
                  pl.BlockSpec((tm, tn), lambda j, i: (i, COL_GATE_A // tn + j)),
                  pl.BlockSpec((tm, tn), lambda j, i: (i, COL_GATE_B // tn + j))],
        out_specs=pl.BlockSpec((tm, tn), lambda j, i: (i, j)),
        out_shape=jax.ShapeDtypeStruct((m, n), BF16),
        scratch_shapes=[pltpu.VMEM((kdim, tn), BF16), pltpu.VMEM((kdim, tn), BF16)],
        compiler_params=_params(("arbitrary", "arbitrary"), blk,
                                2 * kdim * tn * 2 + 4 * tm * tn * 4),
        name="gated_merge",
    )(y_a, y_b, w_a, w_b, proj, proj)


def kernel(x, meta_tokens, norm_mix, w_in, da_q_norm, da_k_norm, da_lambda_q1, da_lambda_k1,
           da_lambda_q2, da_lambda_k2, da_subln, hg_lower_bound, hg_out_norm, w_up_a, w_up_b,
           w_out, norm_mlp, w_ff1, w_ff2):
    batch, seq, d = x.shape
    assert d == D_MODEL and w_in.shape[0] == 1 and w_in.shape[2] == IN_WIDTH
    assert meta_tokens.shape == (N_META, D_MODEL) and seq % 512 == 0
    m = batch * seq
    xf = x.reshape(m, d)
    meta_pad = jnp.concatenate([jnp.zeros((META_ROWS - N_META, d), F32), meta_tokens.astype(F32)])

    u = _rmsnorm(xf, norm_mix[0], 256)
    u_meta = _rmsnorm(meta_pad, norm_mix[0], META_ROWS)
    tn_in = 1024
    group = HEADS * HEAD_W // tn_in
    proj, proj_meta = _matmul_stream(
        u, w_in[0], tm=1024, tn=tn_in, out_dtype=BF16, name="in_proj",
        extra=(u_meta, [(COL_DA_K * HEAD_W // tn_in, 2 * group),
                        (COL_HG_F * HEAD_W // tn_in, 2 * group)]))

    q_gain = jnp.tile(da_q_norm[0], 2) * (QK_DIM ** -0.5 * math.log2(math.e))
    k_gain = jnp.tile(da_k_norm[0], 2)
    pos_meta = jnp.maximum(jnp.arange(META_ROWS, dtype=jnp.int32) - (META_ROWS - N_META), 0)
    tab_main = _rope_tables(jnp.arange(seq, dtype=jnp.int32) + N_META)
    q_hat = _qk_prep(proj, COL_DA_Q, tab_main, q_gain, ATT_BLOCK, seq, 4, False)
    kt = _qk_prep(proj, COL_DA_K, tab_main, k_gain, ATT_BLOCK, seq, 4, True)
    kt_meta = _qk_prep(proj_meta, COL_DA_K, _rope_tables(pos_meta), k_gain, META_ROWS, META_ROWS,
                       4, True)
    lam_params = jnp.stack([da_lambda_q1[0], da_lambda_k1[0], da_lambda_q2[0], da_lambda_k2[0]])
    y_a = _attention(q_hat, kt, proj, kt_meta, proj_meta, lam_params,
                     da_subln[0].reshape(1, HEAD_W), batch, seq, ATT_BLOCK)

    s0 = _hg_meta_state(proj_meta, hg_lower_bound)
    y_b = _hgrn2(proj, s0, hg_lower_bound, hg_out_norm[0].reshape(1, HEAD_W), batch, seq, seq)

    merged = _gated_merge(y_a, y_b, w_up_a[0], w_up_b[0], proj, 1024, 512)
    h1 = _matmul_stream(merged, w_out[0], tm=1024, tn=512, out_dtype=F32, res=xf,
                        name="out_proj")

    v = _rmsnorm(h1, norm_mlp[0], 256)
    hid, w_ff2_b = _matmul_stream(v, w_ff1[0], tm=1024, tn=1024, out_dtype=BF16, relu2=True,
                                  side=w_ff2[0], name="ff1")
    out = _matmul_ktiled(hid, w_ff2_b, h1, tm=1024, tn=1024, tk=4096, name="ff2")
    return out.reshape(batch, seq, d)
```

```python
import functools
import math

import jax
import jax.numpy as jnp
from jax import lax
from jax.experimental import pallas as pl
from jax.experimental.pallas import tpu as pltpu

F32 = jnp.float32
BF16 = jnp.bfloat16

D_MODEL = 4096
N_META = 16
HEADS = 16
HEAD_W = 128
QK_DIM = 64
ROPE_DIMS = 16
ROPE_THETA = 500000.0
HG_CHUNK = 64
META_ROWS = 128
ATT_BLOCK = 512
EPS = 1e-6
NEG_INF = -1e30
LAM_INIT = 0.8 - 0.6 * math.exp(-0.3 * 0)

COL_DA_Q, COL_DA_K, COL_DA_V = 0, 16, 32
COL_HG_Q, COL_HG_F, COL_HG_I, COL_HG_G = 48, 64, 80, 96
COL_GATE_A, COL_GATE_B = 112 * HEAD_W, 144 * HEAD_W
IN_WIDTH = 176 * HEAD_W

V7X_VMEM_LIMIT_CAP = 58 * 2**20


def _params(semantics, block_bytes, extra_bytes=0):
    need = 2 * block_bytes + extra_bytes + (4 << 20)
    return pltpu.CompilerParams(dimension_semantics=semantics,
                                vmem_limit_bytes=min(max(need, 16 << 20), V7X_VMEM_LIMIT_CAP))


def _sigmoid(x):
    return 0.5 * jnp.tanh(0.5 * x) + 0.5


def _rmsnorm_kernel(x_ref, g_ref, o_ref):
    x = x_ref[...]
    r = lax.rsqrt(jnp.mean(x * x, axis=-1, keepdims=True) + EPS)
    o_ref[...] = ((x * r) * g_ref[...]).astype(o_ref.dtype)


def _rmsnorm(x, g, tm):
    m, d = x.shape
    return pl.pallas_call(
        _rmsnorm_kernel,
        grid=(m // tm,),
        in_specs=[pl.BlockSpec((tm, d), lambda i: (i, 0)),
                  pl.BlockSpec((1, d), lambda i: (0, 0))],
        out_specs=pl.BlockSpec((tm, d), lambda i: (i, 0)),
        out_shape=jax.ShapeDtypeStruct((m, d), BF16),
        compiler_params=_params(("parallel",), tm * d * 6),
        name="rmsnorm",
    )(x, g.reshape(1, d))


def _mm_ktiled_kernel(a_ref, w_ref, res_ref, o_ref):
    part = jnp.dot(a_ref[...], w_ref[...], preferred_element_type=F32)
    k = pl.program_id(2)

    @pl.when(k == 0)
    def _():
        o_ref[...] = part + res_ref[...]

    @pl.when(k > 0)
    def _():
        o_ref[...] += part


def _matmul_ktiled(a, w, res, *, tm, tn, tk, name):
    m, kdim = a.shape
    n = w.shape[1]
    tm = min(tm, m)
    block_bytes = tm * tk * 2 + tk * tn * 2 + 2 * tm * tn * 4
    return pl.pallas_call(
        _mm_ktiled_kernel,
        grid=(m // tm, n // tn, kdim // tk),
        in_specs=[pl.BlockSpec((tm, tk), lambda i, j, k: (i, k)),
                  pl.BlockSpec((tk, tn), lambda i, j, k: (k, j)),
                  pl.BlockSpec((tm, tn), lambda i, j, k: (i, j))],
        out_specs=pl.BlockSpec((tm, tn), lambda i, j, k: (i, j)),
        out_shape=jax.ShapeDtypeStruct((m, n), F32),
        compiler_params=_params(("parallel", "parallel", "arbitrary"), block_bytes,
                                2 * tm * tn * 4),
        name=name,
    )(a, w, res)


def _mm_stream_kernel(*refs, relu2, has_res, has_side, extra_cols, row_ssq, emit_norm, n_col,
                      n_row):
    refs = list(refs)
    a_ref, w_hbm = refs.pop(0), refs.pop(0)
    res_ref = refs.pop(0) if has_res else None
    side_hbm = refs.pop(0) if has_side else None
    extra_a_ref = refs.pop(0) if extra_cols else None
    ssq_in_ref = refs.pop(0) if row_ssq else None
    gain_ref = refs.pop(0) if emit_norm else None
    o_ref = refs.pop(0)
    side_out_hbm = refs.pop(0) if has_side else None
    extra_o_ref = refs.pop(0) if extra_cols else None
    scaled_o_ref, ssq_o_ref = (refs.pop(0), refs.pop(0)) if emit_norm else (None, None)
    wb_ref, stage_ref, sem = refs.pop(0), refs.pop(0), refs.pop(0)
    j, i = pl.program_id(0), pl.program_id(1)
    ck, tn = stage_ref.shape[1], stage_ref.shape[2]

    def chunk_copy(col, c, slot):
        return pltpu.make_async_copy(
            w_hbm.at[pl.ds(pl.multiple_of(c * ck, ck), ck), pl.ds(pl.multiple_of(col * tn, tn), tn)],
            stage_ref.at[slot], sem.at[slot])

    def lookahead_col(jj):
        return jnp.minimum(jj + 1, n_col - 1)

    @pl.when((j == 0) & (i == 0))
    def _():
        for c in range(n_row):
            cp = chunk_copy(0, c, c % 2)
            cp.start()
            cp.wait()
            wb_ref[0, c * ck:(c + 1) * ck, :] = stage_ref[c % 2].astype(BF16)
        chunk_copy(lookahead_col(0), 0, 0).start()

    acc = jnp.dot(a_ref[...], wb_ref[j % 2], preferred_element_type=F32)
    if relu2:
        acc = jnp.square(jnp.maximum(acc, 0.0))
    if row_ssq:
        mean_sq = jnp.sum(ssq_in_ref[...], axis=-1, keepdims=True) * (1.0 / a_ref.shape[1])
        acc = acc * (1.0 / (mean_sq + EPS))
    if has_res:
        acc = acc + res_ref[...]
    o_ref[...] = acc.astype(o_ref.dtype)
    if emit_norm:
        scaled_o_ref[...] = (acc * gain_ref[...]).astype(scaled_o_ref.dtype)
        sq = acc * acc
        part = sq[:, :HEAD_W]
        for c in range(1, tn // HEAD_W):
            part = part + sq[:, c * HEAD_W:(c + 1) * HEAD_W]
        ssq_o_ref[...] = part

    chunk_copy(lookahead_col(j), i, i % 2).wait()
    wb_ref[(j + 1) % 2, pl.ds(pl.multiple_of(i * ck, ck), ck), :] = stage_ref[i % 2].astype(BF16)

    @pl.when((j < n_col - 1) | (i < n_row - 1))
    def _():
        wrap = i == n_row - 1
        chunk_copy(lookahead_col(jnp.where(wrap, j + 1, j)), jnp.where(wrap, 0, i + 1),
                   (i + 1) % 2).start()

    if extra_cols:
        wanted = functools.reduce(jnp.logical_or,
                                  [(j >= lo) & (j < lo + cnt) for lo, cnt in extra_cols])

        @pl.when((i == 0) & wanted)
        def _():
            extra_o_ref[...] = jnp.dot(extra_a_ref[...], wb_ref[j % 2],
                                       preferred_element_type=F32).astype(extra_o_ref.dtype)

        @pl.when((i == 0) & jnp.logical_not(wanted))
        def _():
            extra_o_ref[...] = jnp.zeros_like(extra_o_ref)

    if has_side:
        in_buf, out_buf, in_sem, out_sem = refs
        rs = in_buf.shape[1]
        t, n_steps = j * n_row + i, n_col * n_row

        def side_in(tt):
            return pltpu.make_async_copy(side_hbm.at[pl.ds(pl.multiple_of(tt * rs, rs), rs), :],
                                         in_buf.at[tt % 2], in_sem.at[tt % 2])

        def side_out(tt):
            return pltpu.make_async_copy(out_buf.at[tt % 2],
                                         side_out_hbm.at[pl.ds(pl.multiple_of(tt * rs, rs), rs), :],
                                         out_sem.at[tt % 2])

        @pl.when(t == 0)
        def _():
            side_in(t).start()

        @pl.when(t + 1 < n_steps)
        def _():
            side_in(t + 1).start()

        side_in(t).wait()

        @pl.when(t >= 2)
        def _():
            side_out(t - 2).wait()

        out_buf[t % 2] = in_buf[t % 2].astype(BF16)
        side_out(t).start()

        @pl.when(t == n_steps - 1)
        def _():
            side_out(t - 1).wait()
            side_out(t).wait()


def _matmul_stream(a, w, *, tm, tn, out_dtype, relu2=False, res=None, side=None, extra=None,
                   row_ssq=None, norm_gain=None, name):
    m, kdim = a.shape
    n = w.shape[1]
    n_row, n_col = m // tm, n // tn
    assert n_row % 2 == 0 and n_col >= 2 and kdim % n_row == 0
    ck = kdim // n_row
    in_specs = [pl.BlockSpec((tm, kdim), lambda j, i: (i, 0)),
                pl.BlockSpec(memory_space=pl.ANY)]
    args = [a, w]
    out_bytes = jnp.dtype(out_dtype).itemsize
    block_bytes = tm * kdim * 2 + tm * tn * out_bytes
    if res is not None:
        in_specs.append(pl.BlockSpec((tm, tn), lambda j, i: (i, j)))
        args.append(res)
        block_bytes += tm * tn * 4
    out_specs = [pl.BlockSpec((tm, tn), lambda j, i: (i, j))]
    out_shape = [jax.ShapeDtypeStruct((m, n), out_dtype)]
    scratch = [pltpu.VMEM((2, kdim, tn), BF16), pltpu.VMEM((2, ck, tn), F32),
               pltpu.SemaphoreType.DMA((2,))]
    scratch_bytes = 2 * kdim * tn * 2 + 2 * ck * tn * 4
    if side is not None:
        rows, cols = side.shape
        assert rows % (n_row * n_col) == 0
        rs = rows // (n_row * n_col)
        in_specs.append(pl.BlockSpec(memory_space=pl.ANY))
        args.append(side)
        out_specs.append(pl.BlockSpec(memory_space=pl.ANY))
        out_shape.append(jax.ShapeDtypeStruct(side.shape, BF16))
        scratch += [pltpu.VMEM((2, rs, cols), F32), pltpu.VMEM((2, rs, cols), BF16),
                    pltpu.SemaphoreType.DMA((2,)), pltpu.SemaphoreType.DMA((2,))]
        scratch_bytes += 2 * rs * cols * 6
    extra_cols = None
    if extra is not None:
        extra_rows, extra_cols = extra
        r = extra_rows.shape[0]
        in_specs.append(pl.BlockSpec((r, kdim), lambda j, i: (0, 0)))
        args.append(extra_rows)
        out_specs.append(pl.BlockSpec((r, tn), lambda j, i: (0, j)))
        out_shape.append(jax.ShapeDtypeStruct((r, n), BF16))
        block_bytes += r * kdim * 2 + r * tn * 2
    if row_ssq is not None:
        assert relu2
        in_specs.append(pl.BlockSpec((tm, HEAD_W), lambda j, i: (i, 0)))
        args.append(row_ssq)
        block_bytes += tm * HEAD_W * 4
        scratch_bytes += tm * tn * 4
    if norm_gain is not None:
        in_specs.append(pl.BlockSpec((1, tn), lambda j, i: (0, j)))
        args.append(norm_gain.reshape(1, n))
        out_specs += [pl.BlockSpec((tm, tn), lambda j, i: (i, j)),
                      pl.BlockSpec((tm, HEAD_W), lambda j, i: (i, j))]
        out_shape += [jax.ShapeDtypeStruct((m, n), BF16),
                      jax.ShapeDtypeStruct((m, n_col * HEAD_W), F32)]
        block_bytes += tm * tn * 2 + tm * HEAD_W * 4
    results = pl.pallas_call(
        functools.partial(_mm_stream_kernel, relu2=relu2, has_res=res is not None,
                          has_side=side is not None, extra_cols=extra_cols,
                          row_ssq=row_ssq is not None, emit_norm=norm_gain is not None,
                          n_col=n_col, n_row=n_row),
        grid=(n_col, n_row),
        in_specs=in_specs,
        out_specs=out_specs,
        out_shape=out_shape,
        scratch_shapes=scratch,
        compiler_params=_params(("arbitrary", "arbitrary"), block_bytes,
                                scratch_bytes + tm * tn * 4),
        name=name,
    )(*args)
    return results[0] if len(results) == 1 else tuple(results)


def _fold_lanes_kernel(x_ref, o_ref):
    x = x_ref[...]
    acc = x[:, :HEAD_W]
    for c in range(1, x.shape[1] // HEAD_W):
        acc = acc + x[:, c * HEAD_W:(c + 1) * HEAD_W]
    o_ref[...] = acc


def _fold_lanes(x, tm):
    m, w = x.shape
    return pl.pallas_call(
        _fold_lanes_kernel,
        grid=(m // tm,),
        in_specs=[pl.BlockSpec((tm, w), lambda i: (i, 0))],
        out_specs=pl.BlockSpec((tm, HEAD_W), lambda i: (i, 0)),
        out_shape=jax.ShapeDtypeStruct((m, HEAD_W), F32),
        compiler_params=_params(("parallel",), tm * (w + HEAD_W) * 4),
        name="fold_lanes",
    )(x)


def _split_terms(x, n):
    terms = []
    for _ in range(n - 1):
        t = x.astype(BF16)
        terms.append(t)
        x = x - t.astype(F32)
    terms.append(x.astype(BF16))
    return terms


def _qk_prep_kernel(x_ref, cos_ref, sin_ref, g_ref, ones_ref, swap_ref, o_ref, *,
                    heads_per_block, transpose_out):
    g, cos, sin = g_ref[...], cos_ref[...], sin_ref[...]
    for hh in range(heads_per_block):
        sl = slice(hh * HEAD_W, (hh + 1) * HEAD_W)
        x = x_ref[:, sl].astype(F32)
        ss = jnp.dot((x * x).astype(BF16), ones_ref[...], preferred_element_type=F32)
        y = (x * lax.rsqrt(ss * (1.0 / QK_DIM) + EPS)) * g
        partner = jnp.dot(jnp.concatenate(_split_terms(y, 2), axis=1), swap_ref[...],
                          preferred_element_type=F32)
        y = y * cos + partner * sin
        if transpose_out:
            o_ref[0, hh, 0] = y.T.astype(o_ref.dtype)
        else:
            o_ref[:, sl] = y.astype(o_ref.dtype)


def _qk_prep(proj, col0, tables, gain, tm, rows_per_seq, heads_per_block, transpose_out):
    m = proj.shape[0]
    nblk = rows_per_seq // tm
    tw = heads_per_block * HEAD_W
    lane_group = jnp.arange(HEAD_W) // QK_DIM
    group_ones = (lane_group[:, None] == lane_group[None, :]).astype(BF16)
    tab_spec = pl.BlockSpec((tm, HEAD_W), lambda i, j: (i % nblk, 0))
    lanes = jnp.arange(HEAD_W)
    in_rope = lanes % QK_DIM < ROPE_DIMS
    first_half = lanes % QK_DIM < ROPE_DIMS // 2
    partner_lane = jnp.where(first_half, lanes + ROPE_DIMS // 2, lanes - ROPE_DIMS // 2)
    swap = jnp.where(in_rope[None, :] & (lanes[:, None] == partner_lane[None, :]),
                     jnp.where(first_half, -1.0, 1.0)[None, :], 0.0).astype(BF16)
    swap = jnp.tile(swap, (2, 1))
    if transpose_out:
        out_spec = pl.BlockSpec((1, heads_per_block, 1, HEAD_W, tm),
                                lambda i, j: (i // nblk, j, i % nblk, 0, 0))
        out_shape = jax.ShapeDtypeStruct((m // rows_per_seq, HEADS, nblk, HEAD_W, tm), BF16)
    else:
        out_spec = pl.BlockSpec((tm, tw), lambda i, j: (i, j))
        out_shape = jax.ShapeDtypeStruct((m, HEADS * HEAD_W), BF16)
    return pl.pallas_call(
        functools.partial(_qk_prep_kernel, heads_per_block=heads_per_block,
                          transpose_out=transpose_out),
        grid=(m // tm, HEADS // heads_per_block),
        in_specs=[pl.BlockSpec((tm, tw), lambda i, j: (i, col0 * HEAD_W // tw + j)),
                  tab_spec, tab_spec,
                  pl.BlockSpec((1, HEAD_W), lambda i, j: (0, 0)),
                  pl.BlockSpec((HEAD_W, HEAD_W), lambda i, j: (0, 0)),
                  pl.BlockSpec((2 * HEAD_W, HEAD_W), lambda i, j: (0, 0))],
        out_specs=out_spec,
        out_shape=out_shape,
        compiler_params=_params(("parallel", "parallel"), tm * tw * 4 + 3 * tm * HEAD_W * 4,
                                tm * HEAD_W * 48),
        name="k_prep" if transpose_out else "q_prep",
    )(proj, *tables, gain.reshape(1, HEAD_W), group_ones, swap)


def _rope_tables(pos):
    half = ROPE_DIMS // 2
    inv_freq = ROPE_THETA ** (-(jnp.arange(half, dtype=F32) * 2.0) / ROPE_DIMS)
    ang = pos.astype(F32)[:, None] * inv_freq[None, :]
    cos, sin = jnp.cos(ang), jnp.sin(ang)
    n = pos.shape[0]
    pad = QK_DIM - ROPE_DIMS
    cos_t = jnp.concatenate([cos, cos, jnp.ones((n, pad), F32)], axis=1)
    sin_t = jnp.concatenate([sin, sin, jnp.zeros((n, pad), F32)], axis=1)
    return jnp.tile(cos_t, (1, 2)), jnp.tile(sin_t, (1, 2))


def _nt_dot(a, b):
    return lax.dot_general(a, b, (((1,), (1,)), ((), ())), preferred_element_type=F32)


def _lane_repeat(x, n):
    return jnp.concatenate([x] * n, axis=1)


def _attn_kernel(q_ref, kt_ref, v_ref, ktm_ref, vm_ref, lam_ref, subln_ref, o_ref,
                 vp_ref, vpm_ref, *, tq, nq):
    ones = jnp.ones((vp_ref.shape[0], HEAD_W), BF16)
    vp_ref[:, :HEAD_W] = v_ref[...]
    vp_ref[:, HEAD_W:] = ones
    vpm_ref[:, :HEAD_W] = vm_ref[...]
    vpm_ref[:, HEAD_W:] = ones[:META_ROWS]

    lp = lam_ref[...]
    lam = (jnp.exp(jnp.sum(lp[0:1] * lp[1:2], axis=-1, keepdims=True))
           - jnp.exp(jnp.sum(lp[2:3] * lp[3:4], axis=-1, keepdims=True)) + LAM_INIT)
    lane = lax.broadcasted_iota(jnp.int32, (tq, HEAD_W), 1)
    col_m = lax.broadcasted_iota(jnp.int32, (tq, META_ROWS), 1)
    row = lax.broadcasted_iota(jnp.int32, (tq, tq), 0)
    col = lax.broadcasted_iota(jnp.int32, (tq, tq), 1)

    def q_maps(qv):
        q = q_ref[qv * tq:(qv + 1) * tq, :]
        zero = jnp.zeros_like(q)
        return (jnp.where(lane < QK_DIM, q, zero),
                jnp.where(lane >= QK_DIM, q, zero))

    def scores(task):
        qv, j = task
        kt = ktm_ref[0, 0, 0] if j is None else kt_ref[0, 0, j]
        return [jnp.dot(qc, kt, preferred_element_type=F32) for qc in q_maps(qv)]

    def consume(state, s_maps, vp, mask):
        out = []
        for (m_old, acc), s in zip(state, s_maps):
            if mask is not None:
                s = jnp.where(mask, s, NEG_INF)
            m_new = jnp.maximum(m_old, jnp.max(s, axis=-1, keepdims=True))
            p = jnp.exp2((s - _lane_repeat(m_new, s.shape[1] // HEAD_W)).astype(BF16))
            alpha = jnp.exp2(m_old - m_new)
            acc = (_lane_repeat(alpha, 2) * acc
                   + jnp.dot(p, vp, preferred_element_type=F32))
            out.append((m_new, acc))
        return out

    def finalize(qv, state):
        a1, a2 = state[0][1], state[1][1]
        o = (a1[:, :HEAD_W] / a1[:, HEAD_W:HEAD_W + 1]
             - lam * (a2[:, :HEAD_W] / a2[:, HEAD_W:HEAD_W + 1]))
        r = lax.rsqrt(jnp.mean(o * o, axis=-1, keepdims=True) + EPS)
        o_ref[qv * tq:(qv + 1) * tq, :] = (
            ((o * r) * subln_ref[...]) * (1.0 - LAM_INIT)).astype(o_ref.dtype)

    tasks = [(qv, j) for qv in range(nq) for j in [None] + list(range(qv + 1))]
    s_next = scores(tasks[0])
    state = None
    for t, (qv, j) in enumerate(tasks):
        s_cur = s_next
        if t + 1 < len(tasks):
            s_next = scores(tasks[t + 1])
        if j is None:
            state = [(jnp.full((tq, HEAD_W), NEG_INF, F32), jnp.zeros((tq, 2 * HEAD_W), F32))] * 2
            state = consume(state, s_cur, vpm_ref[...], col_m >= META_ROWS - N_META)
        else:
            state = consume(state, s_cur, vp_ref[j * tq:(j + 1) * tq, :],
                            col <= row if j == qv else None)
            if j == qv:
                finalize(qv, state)


def _attention(q, kt, proj, kt_meta, proj_meta, lam_params, subln, batch, seq, tq):
    nq = seq // tq
    kernel = functools.partial(_attn_kernel, tq=tq, nq=nq)
    blk = 4 * seq * HEAD_W * 2 + 2 * META_ROWS * HEAD_W * 2
    scratch_bytes = (seq + META_ROWS) * 2 * HEAD_W * 2
    return pl.pallas_call(
        kernel,
        grid=(batch, HEADS),
        in_specs=[pl.BlockSpec((seq, HEAD_W), lambda b, h: (b, h)),
                  pl.BlockSpec((1, 1, nq, HEAD_W, tq), lambda b, h: (b, h, 0, 0, 0)),
                  pl.BlockSpec((seq, HEAD_W), lambda b, h: (b, COL_DA_V + h)),
                  pl.BlockSpec((1, 1, 1, HEAD_W, META_ROWS), lambda b, h: (0, h, 0, 0, 0)),
                  pl.BlockSpec((META_ROWS, HEAD_W), lambda b, h: (0, COL_DA_V + h)),
                  pl.BlockSpec((4, QK_DIM), lambda b, h: (0, 0)),
                  pl.BlockSpec((1, HEAD_W), lambda b, h: (0, 0))],
        out_specs=pl.BlockSpec((seq, HEAD_W), lambda b, h: (b, h)),
        out_shape=jax.ShapeDtypeStruct((batch * seq, HEADS * HEAD_W), BF16),
        scratch_shapes=[pltpu.VMEM((seq, 2 * HEAD_W), BF16),
                        pltpu.VMEM((META_ROWS, 2 * HEAD_W), BF16)],
        compiler_params=_params(("parallel", "parallel"), blk, scratch_bytes + 16 * tq * tq * 4),
        name="diff_attention",
    )(q, kt, proj, kt_meta, proj_meta, lam_params, subln)


def _lower_bound(lb_ref):
    x = lb_ref[...]
    e = jnp.exp(x - jnp.max(x, axis=0, keepdims=True))
    return e[0:1] / jnp.sum(e, axis=0, keepdims=True)


def _chunk_tri():
    r = jnp.arange(HG_CHUNK)
    return (r[None, :] <= r[:, None]).astype(BF16)


def _chunk_cumsum(tri, g):
    n = g.shape[0]
    parts = jnp.concatenate(_split_terms(g, 2), axis=2)
    s = lax.dot_general(jnp.broadcast_to(tri, (n,) + tri.shape), parts,
                        (((2,), (1,)), ((0,), (0,))), preferred_element_type=F32)
    return s[:, :, :HEAD_W] + s[:, :, HEAD_W:]


def _hg_block(q, fl, v, lb, tri, st, valid=None):
    c = HG_CHUNK
    n = q.shape[0] // c
    f = lb + (1.0 - lb) * _sigmoid(fl)
    g = jnp.log(f)
    kk = 1.0 - f
    if valid is not None:
        g = jnp.where(valid, g, 0.0)
        kk = jnp.where(valid, kk, 0.0)
        v = jnp.where(valid, v, jnp.zeros_like(v))
    b = _chunk_cumsum(tri, g.reshape(n, c, HEAD_W))
    q3, k3, v3 = q.reshape(n, c, HEAD_W), kk.reshape(n, c, HEAD_W), v.reshape(n, c, HEAD_W)
    b_mid = b[:, c // 2 - 1:c // 2, :]
    b_last = b[:, c - 1:c, :]
    q_in = (q3 * jnp.exp(b - b_mid)).astype(BF16)
    k_in = (k3 * jnp.exp(b_mid - b)).astype(BF16)
    k_dec = (k3 * jnp.exp(b_last - b)).astype(BF16)
    q_ex = (q3 * jnp.exp(b)).astype(BF16)
    decay = jnp.exp(b_last)
    row = lax.broadcasted_iota(jnp.int32, (n, c, c), 1)
    col = lax.broadcasted_iota(jnp.int32, (n, c, c), 2)
    a = lax.dot_general(q_in, k_in, (((2,), (2,)), ((0,), (0,))), preferred_element_type=F32)
    a = jnp.where(col <= row, a, 0.0).astype(BF16)
    o_intra = lax.dot_general(a, v3, (((2,), (1,)), ((0,), (0,))), preferred_element_type=F32)
    u_t = [lax.dot_general(v3[i], k_dec[i], (((0,), (0,)), ((), ())), preferred_element_type=F32)
           for i in range(n)]
    outs = []
    for i in range(n):
        outs.append(o_intra[i] + _nt_dot(q_ex[i], st.astype(BF16)))
        st = st * decay[i] + u_t[i]
    return outs, st


def _hg_meta_kernel(f_ref, i_ref, tri_ref, lb_ref, s_ref):
    row = lax.broadcasted_iota(jnp.int32, (HG_CHUNK, HEAD_W), 0)
    _, st = _hg_block(jnp.zeros((HG_CHUNK, HEAD_W), F32), f_ref[...].astype(F32), i_ref[...],
                      _lower_bound(lb_ref), tri_ref[...], jnp.zeros((HEAD_W, HEAD_W), F32),
                      valid=row >= HG_CHUNK - N_META)
    s_ref[0] = st


def _hg_meta_state(proj_meta, lower_bound):
    blk = META_ROWS // HG_CHUNK - 1
    return pl.pallas_call(
        _hg_meta_kernel,
        grid=(HEADS,),
        in_specs=[pl.BlockSpec((HG_CHUNK, HEAD_W), lambda h: (blk, COL_HG_F + h)),
                  pl.BlockSpec((HG_CHUNK, HEAD_W), lambda h: (blk, COL_HG_I + h)),
                  pl.BlockSpec((HG_CHUNK, HG_CHUNK), lambda h: (0, 0)),
                  pl.BlockSpec((lower_bound.shape[0], HEAD_W), lambda h: (0, h))],
        out_specs=pl.BlockSpec((1, HEAD_W, HEAD_W), lambda h: (h, 0, 0)),
        out_shape=jax.ShapeDtypeStruct((HEADS, HEAD_W, HEAD_W), F32),
        compiler_params=_params(("parallel",), 1 << 20),
        name="hgrn2_meta_state",
    )(proj_meta, proj_meta, _chunk_tri(), lower_bound)


def _hgrn_kernel(q_ref, f_ref, i_ref, g_ref, tri_ref, lb_ref, s0_ref, on_ref, o_ref, st_ref):
    @pl.when(pl.program_id(2) == 0)
    def _():
        st_ref[...] = s0_ref[0]

    outs, st = _hg_block(q_ref[...].astype(F32), f_ref[...].astype(F32), i_ref[...],
                         _lower_bound(lb_ref), tri_ref[...], st_ref[...])
    st_ref[...] = st
    for i, o in enumerate(outs):
        sl = pl.ds(i * HG_CHUNK, HG_CHUNK)
        r = lax.rsqrt(jnp.mean(o * o, axis=-1, keepdims=True) + EPS)
        y = ((o * r) * on_ref[...]) * _sigmoid(g_ref[sl, :].astype(F32))
        o_ref[sl, :] = y.astype(o_ref.dtype)


def _hgrn2(proj, s0, lower_bound, out_norm, batch, seq, tb):
    nt = seq // tb

    def col(off):
        return pl.BlockSpec((tb, HEAD_W), lambda b, h, t: (b * nt + t, off + h))

    return pl.pallas_call(
        _hgrn_kernel,
        grid=(batch, HEADS, nt),
        in_specs=[col(COL_HG_Q), col(COL_HG_F), col(COL_HG_I), col(COL_HG_G),
                  pl.BlockSpec((HG_CHUNK, HG_CHUNK), lambda b, h, t: (0, 0)),
                  pl.BlockSpec((lower_bound.shape[0], HEAD_W), lambda b, h, t: (0, h)),
                  pl.BlockSpec((1, HEAD_W, HEAD_W), lambda b, h, t: (h, 0, 0)),
                  pl.BlockSpec((1, HEAD_W), lambda b, h, t: (0, 0))],
        out_specs=pl.BlockSpec((tb, HEAD_W), lambda b, h, t: (b * nt + t, h)),
        out_shape=jax.ShapeDtypeStruct((batch * seq, HEADS * HEAD_W), BF16),
        scratch_shapes=[pltpu.VMEM((HEAD_W, HEAD_W), F32)],
        compiler_params=_params(("parallel", "parallel", "arbitrary"),
                                5 * tb * HEAD_W * 2, 24 * tb * HEAD_W * 4),
        name="hgrn2",
    )(proj, proj, proj, proj, _chunk_tri(), lower_bound, s0, out_norm)


def _merge_kernel(ya_ref, yb_ref, wa_ref, wb_ref, ga_ref, gb_ref, o_ref, wa_bf_ref, wb_bf_ref):
    @pl.when(pl.program_id(1) == 0)
    def _():
        wa_bf_ref[...] = wa_ref[...].astype(BF16)
        wb_bf_ref[...] = wb_ref[...].astype(BF16)

    a = jnp.dot(ya_ref[...], wa_bf_ref[...], preferred_element_type=F32)
    b = jnp.dot(yb_ref[...], wb_bf_ref[...], preferred_element_type=F32)
    o = (_sigmoid(ga_ref[...].astype(F32)) * a + _sigmoid(gb_ref[...].astype(F32)) * b)
    o_ref[...] = o.astype(o_ref.dtype)


def _gated_merge(y_a, y_b, w_a, w_b, proj, tm, tn):
    m = y_a.shape[0]
    kdim, n = w_a.shape
    tm = min(tm, m)
    blk = 2 * tm * kdim * 2 + 2 * kdim * tn * 4 + 3 * tm * tn * 2
    return pl.pallas_call(
        _merge_kernel,
        grid=(n // tn, m // tm),
        in_specs=[pl.BlockSpec((tm, kdim), lambda j, i: (i, 0)),
                  pl.BlockSpec((tm, kdim), lambda j, i: (i, 0)),
                  pl.BlockSpec((kdim, tn), lambda j, i: (0, j)),
                  pl.BlockSpec((kdim, tn), lambda j, i: (0, j)),
                  pl.BlockSpec((tm, tn), lambda j, i: (i, COL_GATE_A // tn + j)),
                  pl.BlockSpec((tm, tn), lambda j, i: (i, COL_GATE_B // tn + j))],
        out_specs=pl.BlockSpec((tm, tn), lambda j, i: (i, j)),
        out_shape=jax.ShapeDtypeStruct((m, n), BF16),
        scratch_shapes=[pltpu.VMEM((kdim, tn), BF16), pltpu.VMEM((kdim, tn), BF16)],
        compiler_params=_params(("arbitrary", "arbitrary"), blk,
                                2 * kdim * tn * 2 + 4 * tm * tn * 4),
        name="gated_merge",
    )(y_a, y_b, w_a, w_b, proj, proj)


def kernel(x, meta_tokens, norm_mix, w_in, da_q_norm, da_k_norm, da_lambda_q1, da_lambda_k1,
           da_lambda_q2, da_lambda_k2, da_subln, hg_lower_bound, hg_out_norm, w_up_a, w_up_b,
           w_out, norm_mlp, w_ff1, w_ff2):
    batch, seq, d = x.shape
    assert d == D_MODEL and w_in.shape[0] == 1 and w_in.shape[2] == IN_WIDTH
    assert meta_tokens.shape == (N_META, D_MODEL) and seq % 512 == 0
    m = batch * seq
    xf = x.reshape(m, d)
    meta_pad = jnp.concatenate([jnp.zeros((META_ROWS - N_META, d), F32), meta_tokens.astype(F32)])

    u = _rmsnorm(xf, norm_mix[0], 256)
    u_meta = _rmsnorm(meta_pad, norm_mix[0], META_ROWS)
    tn_in = 1024
    group = HEADS * HEAD_W // tn_in
    proj, proj_meta = _matmul_stream(
        u, w_in[0], tm=1024, tn=tn_in, out_dtype=BF16, name="in_proj",
        extra=(u_meta, [(COL_DA_K * HEAD_W // tn_in, 2 * group),
                        (COL_HG_F * HEAD_W // tn_in, 2 * group)]))

    q_gain = jnp.tile(da_q_norm[0], 2) * (QK_DIM ** -0.5 * math.log2(math.e))
    k_gain = jnp.tile(da_k_norm[0], 2)
    pos_meta = jnp.maximum(jnp.arange(META_ROWS, dtype=jnp.int32) - (META_ROWS - N_META), 0)
    tab_main = _rope_tables(jnp.arange(seq, dtype=jnp.int32) + N_META)
    q_hat = _qk_prep(proj, COL_DA_Q, tab_main, q_gain, ATT_BLOCK, seq, 4, False)
    kt = _qk_prep(proj, COL_DA_K, tab_main, k_gain, ATT_BLOCK, seq, 4, True)
    kt_meta = _qk_prep(proj_meta, COL_DA_K, _rope_tables(pos_meta), k_gain, META_ROWS, META_ROWS,
                       4, True)
    lam_params = jnp.stack([da_lambda_q1[0], da_lambda_k1[0], da_lambda_q2[0], da_lambda_k2[0]])
    y_a = _attention(q_hat, kt, proj, kt_meta, proj_meta, lam_params,
                     da_subln[0].reshape(1, HEAD_W), batch, seq, ATT_BLOCK)

    s0 = _hg_meta_state(proj_meta, hg_lower_bound)
    y_b = _hgrn2(proj, s0, hg_lower_bound, hg_out_norm[0].reshape(1, HEAD_W), batch, seq, seq)

    merged = _gated_merge(y_a, y_b, w_up_a[0], w_up_b[0], proj, 1024, 512)
    h1, h1_gained, h1_ssq = _matmul_stream(merged, w_out[0], tm=1024, tn=512, out_dtype=F32, res=xf,
                                           norm_gain=norm_mlp[0], name="out_proj")
    h1_ssq = _fold_lanes(h1_ssq, min(1024, m))

    hid, w_ff2_b = _matmul_stream(h1_gained, w_ff1[0], tm=1024, tn=1024, out_dtype=BF16, relu2=True,
                                  side=w_ff2[0], row_ssq=h1_ssq, name="ff1")
    out = _matmul_ktiled(hid, w_ff2_b, h1, tm=1024, tn=1024, tk=4096, name="ff2")
    return out.reshape(batch, seq, d)
```

```python
import functools
import math

import jax
import jax.numpy as jnp
from jax import lax
from jax.experimental import pallas as pl
from jax.experimental.pallas import tpu as pltpu

F32 = jnp.float32
BF16 = jnp.bfloat16

D_MODEL = 4096
N_META = 16
HEADS = 16
HEAD_W = 128
QK_DIM = 64
ROPE_DIMS = 16
ROPE_THETA = 500000.0
HG_CHUNK = 64
META_ROWS = 128
ATT_BLOCK = 512
EPS = 1e-6
NEG_INF = -1e30
LAM_INIT = 0.8 - 0.6 * math.exp(-0.3 * 0)

COL_DA_Q, COL_DA_K, COL_DA_V = 0, 16, 32
COL_HG_Q, COL_HG_F, COL_HG_I, COL_HG_G = 48, 64, 80, 96
COL_GATE_A, COL_GATE_B = 112 * HEAD_W, 144 * HEAD_W
IN_WIDTH = 176 * HEAD_W

V7X_VMEM_LIMIT_CAP = 58 * 2**20


def _params(semantics, block_bytes, extra_bytes=0):
    need = 2 * block_bytes + extra_bytes + (4 << 20)
    return pltpu.CompilerParams(dimension_semantics=semantics,
                                vmem_limit_bytes=min(max(need, 16 << 20), V7X_VMEM_LIMIT_CAP))


def _sigmoid(x):
    return 0.5 * jnp.tanh(0.5 * x) + 0.5


def _rmsnorm_kernel(x_ref, g_ref, o_ref):
    x = x_ref[...]
    r = lax.rsqrt(jnp.mean(x * x, axis=-1, keepdims=True) + EPS)
    o_ref[...] = ((x * r) * g_ref[...]).astype(o_ref.dtype)


def _rmsnorm(x, g, tm):
    m, d = x.shape
    return pl.pallas_call(
        _rmsnorm_kernel,
        grid=(m // tm,),
        in_specs=[pl.BlockSpec((tm, d), lambda i: (i, 0)),
                  pl.BlockSpec((1, d), lambda i: (0, 0))],
        out_specs=pl.BlockSpec((tm, d), lambda i: (i, 0)),
        out_shape=jax.ShapeDtypeStruct((m, d), BF16),
        compiler_params=_params(("parallel",), tm * d * 6, 2 * tm * d * 4),
        name="rmsnorm",
    )(x, g.reshape(1, d))


def _mm_ktiled_kernel(a_ref, w_ref, res_ref, o_ref):
    part = jnp.dot(a_ref[...], w_ref[...], preferred_element_type=F32)
    k = pl.program_id(2)

    @pl.when(k == 0)
    def _():
        o_ref[...] = part + res_ref[...]

    @pl.when(k > 0)
    def _():
        o_ref[...] += part


def _matmul_ktiled(a, w, res, *, tm, tn, tk, name):
    m, kdim = a.shape
    n = w.shape[1]
    tm = min(tm, m)
    block_bytes = tm * tk * 2 + tk * tn * 2 + 2 * tm * tn * 4
    return pl.pallas_call(
        _mm_ktiled_kernel,
        grid=(m // tm, n // tn, kdim // tk),
        in_specs=[pl.BlockSpec((tm, tk), lambda i, j, k: (i, k)),
                  pl.BlockSpec((tk, tn), lambda i, j, k: (k, j)),
                  pl.BlockSpec((tm, tn), lambda i, j, k: (i, j))],
        out_specs=pl.BlockSpec((tm, tn), lambda i, j, k: (i, j)),
        out_shape=jax.ShapeDtypeStruct((m, n), F32),
        compiler_params=_params(("parallel", "parallel", "arbitrary"), block_bytes,
                                2 * tm * tn * 4),
        name=name,
    )(a, w, res)


def _mm_stream_kernel(*refs, relu2, has_res, has_side, extra_cols, row_ssq, emit_norm, n_col,
                      n_row):
    refs = list(refs)
    a_ref, w_hbm = refs.pop(0), refs.pop(0)
    res_ref = refs.pop(0) if has_res else None
    side_hbm = refs.pop(0) if has_side else None
    extra_a_ref = refs.pop(0) if extra_cols else None
    ssq_in_ref = refs.pop(0) if row_ssq else None
    gain_ref = refs.pop(0) if emit_norm else None
    o_ref = refs.pop(0)
    side_out_hbm = refs.pop(0) if has_side else None
    extra_o_ref = refs.pop(0) if extra_cols else None
    scaled_o_ref, ssq_o_ref = (refs.pop(0), refs.pop(0)) if emit_norm else (None, None)
    wb_ref, stage_ref, sem = refs.pop(0), refs.pop(0), refs.pop(0)
    j, i = pl.program_id(0), pl.program_id(1)
    ck, tn = stage_ref.shape[1], stage_ref.shape[2]

    def chunk_copy(col, c, slot):
        return pltpu.make_async_copy(
            w_hbm.at[pl.ds(pl.multiple_of(c * ck, ck), ck), pl.ds(pl.multiple_of(col * tn, tn), tn)],
            stage_ref.at[slot], sem.at[slot])

    def lookahead_col(jj):
        return jnp.minimum(jj + 1, n_col - 1)

    @pl.when((j == 0) & (i == 0))
    def _():
        for c in range(n_row):
            cp = chunk_copy(0, c, c % 2)
            cp.start()
            cp.wait()
            wb_ref[0, c * ck:(c + 1) * ck, :] = stage_ref[c % 2].astype(BF16)
        chunk_copy(lookahead_col(0), 0, 0).start()

    acc = jnp.dot(a_ref[...], wb_ref[j % 2], preferred_element_type=F32)
    if relu2:
        acc = jnp.square(jnp.maximum(acc, 0.0))
    if row_ssq:
        mean_sq = jnp.sum(ssq_in_ref[...], axis=-1, keepdims=True) * (1.0 / a_ref.shape[1])
        acc = acc * (1.0 / (mean_sq + EPS))
    if has_res:
        acc = acc + res_ref[...]
    o_ref[...] = acc.astype(o_ref.dtype)
    if emit_norm:
        scaled_o_ref[...] = (acc * gain_ref[...]).astype(scaled_o_ref.dtype)
        sq = acc * acc
        part = sq[:, :HEAD_W]
        for c in range(1, tn // HEAD_W):
            part = part + sq[:, c * HEAD_W:(c + 1) * HEAD_W]
        ssq_o_ref[...] = part

    chunk_copy(lookahead_col(j), i, i % 2).wait()
    wb_ref[(j + 1) % 2, pl.ds(pl.multiple_of(i * ck, ck), ck), :] = stage_ref[i % 2].astype(BF16)

    @pl.when((j < n_col - 1) | (i < n_row - 1))
    def _():
        wrap = i == n_row - 1
        chunk_copy(lookahead_col(jnp.where(wrap, j + 1, j)), jnp.where(wrap, 0, i + 1),
                   (i + 1) % 2).start()

    if extra_cols:
        wanted = functools.reduce(jnp.logical_or,
                                  [(j >= lo) & (j < lo + cnt) for lo, cnt in extra_cols])

        @pl.when((i == 0) & wanted)
        def _():
            extra_o_ref[...] = jnp.dot(extra_a_ref[...], wb_ref[j % 2],
                                       preferred_element_type=F32).astype(extra_o_ref.dtype)

        @pl.when((i == 0) & jnp.logical_not(wanted))
        def _():
            extra_o_ref[...] = jnp.zeros_like(extra_o_ref)

    if has_side:
        in_buf, out_buf, in_sem, out_sem = refs
        rs = in_buf.shape[1]
        t, n_steps = j * n_row + i, n_col * n_row

        def side_in(tt):
            return pltpu.make_async_copy(side_hbm.at[pl.ds(pl.multiple_of(tt * rs, rs), rs), :],
                                         in_buf.at[tt % 2], in_sem.at[tt % 2])

        def side_out(tt):
            return pltpu.make_async_copy(out_buf.at[tt % 2],
                                         side_out_hbm.at[pl.ds(pl.multiple_of(tt * rs, rs), rs), :],
                                         out_sem.at[tt % 2])

        @pl.when(t == 0)
        def _():
            side_in(t).start()

        @pl.when(t + 1 < n_steps)
        def _():
            side_in(t + 1).start()

        side_in(t).wait()

        @pl.when(t >= 2)
        def _():
            side_out(t - 2).wait()

        out_buf[t % 2] = in_buf[t % 2].astype(BF16)
        side_out(t).start()

        @pl.when(t == n_steps - 1)
        def _():
            side_out(t - 1).wait()
            side_out(t).wait()


def _matmul_stream(a, w, *, tm, tn, out_dtype, relu2=False, res=None, side=None, extra=None,
                   row_ssq=None, norm_gain=None, name):
    m, kdim = a.shape
    n = w.shape[1]
    n_row, n_col = m // tm, n // tn
    assert n_row % 2 == 0 and n_col >= 2 and kdim % n_row == 0
    ck = kdim // n_row
    in_specs = [pl.BlockSpec((tm, kdim), lambda j, i: (i, 0)),
                pl.BlockSpec(memory_space=pl.ANY)]
    args = [a, w]
    out_bytes = jnp.dtype(out_dtype).itemsize
    block_bytes = tm * kdim * 2 + tm * tn * out_bytes
    if res is not None:
        in_specs.append(pl.BlockSpec((tm, tn), lambda j, i: (i, j)))
        args.append(res)
        block_bytes += tm * tn * 4
    out_specs = [pl.BlockSpec((tm, tn), lambda j, i: (i, j))]
    out_shape = [jax.ShapeDtypeStruct((m, n), out_dtype)]
    scratch = [pltpu.VMEM((2, kdim, tn), BF16), pltpu.VMEM((2, ck, tn), F32),
               pltpu.SemaphoreType.DMA((2,))]
    scratch_bytes = 2 * kdim * tn * 2 + 2 * ck * tn * 4
    if side is not None:
        rows, cols = side.shape
        assert rows % (n_row * n_col) == 0
        rs = rows // (n_row * n_col)
        in_specs.append(pl.BlockSpec(memory_space=pl.ANY))
        args.append(side)
        out_specs.append(pl.BlockSpec(memory_space=pl.ANY))
        out_shape.append(jax.ShapeDtypeStruct(side.shape, BF16))
        scratch += [pltpu.VMEM((2, rs, cols), F32), pltpu.VMEM((2, rs, cols), BF16),
                    pltpu.SemaphoreType.DMA((2,)), pltpu.SemaphoreType.DMA((2,))]
        scratch_bytes += 2 * rs * cols * 6
    extra_cols = None
    if extra is not None:
        extra_rows, extra_cols = extra
        r = extra_rows.shape[0]
        in_specs.append(pl.BlockSpec((r, kdim), lambda j, i: (0, 0)))
        args.append(extra_rows)
        out_specs.append(pl.BlockSpec((r, tn), lambda j, i: (0, j)))
        out_shape.append(jax.ShapeDtypeStruct((r, n), BF16))
        block_bytes += r * kdim * 2 + r * tn * 2
    if row_ssq is not None:
        assert relu2
        in_specs.append(pl.BlockSpec((tm, HEAD_W), lambda j, i: (i, 0)))
        args.append(row_ssq)
        block_bytes += tm * HEAD_W * 4
        scratch_bytes += tm * tn * 4
    if norm_gain is not None:
        in_specs.append(pl.BlockSpec((1, tn), lambda j, i: (0, j)))
        args.append(norm_gain.reshape(1, n))
        out_specs += [pl.BlockSpec((tm, tn), lambda j, i: (i, j)),
                      pl.BlockSpec((tm, HEAD_W), lambda j, i: (i, j))]
        out_shape += [jax.ShapeDtypeStruct((m, n), BF16),
                      jax.ShapeDtypeStruct((m, n_col * HEAD_W), F32)]
        block_bytes += tm * tn * 2 + tm * HEAD_W * 4
    results = pl.pallas_call(
        functools.partial(_mm_stream_kernel, relu2=relu2, has_res=res is not None,
                          has_side=side is not None, extra_cols=extra_cols,
                          row_ssq=row_ssq is not None, emit_norm=norm_gain is not None,
                          n_col=n_col, n_row=n_row),
        grid=(n_col, n_row),
        in_specs=in_specs,
        out_specs=out_specs,
        out_shape=out_shape,
        scratch_shapes=scratch,
        compiler_params=_params(("arbitrary", "arbitrary"), block_bytes,
                                scratch_bytes + tm * tn * 4),
        name=name,
    )(*args)
    return results[0] if len(results) == 1 else tuple(results)


def _fold_lanes_kernel(x_ref, o_ref):
    x = x_ref[...]
    acc = x[:, :HEAD_W]
    for c in range(1, x.shape[1] // HEAD_W):
        acc = acc + x[:, c * HEAD_W:(c + 1) * HEAD_W]
    o_ref[...] = acc


def _fold_lanes(x, tm):
    m, w = x.shape
    return pl.pallas_call(
        _fold_lanes_kernel,
        grid=(m // tm,),
        in_specs=[pl.BlockSpec((tm, w), lambda i: (i, 0))],
        out_specs=pl.BlockSpec((tm, HEAD_W), lambda i: (i, 0)),
        out_shape=jax.ShapeDtypeStruct((m, HEAD_W), F32),
        compiler_params=_params(("parallel",), tm * (w + HEAD_W) * 4),
        name="fold_lanes",
    )(x)


def _split_terms(x, n):
    terms = []
    for _ in range(n - 1):
        t = x.astype(BF16)
        terms.append(t)
        x = x - t.astype(F32)
    terms.append(x.astype(BF16))
    return terms


def _qk_prep_kernel(x_ref, cos_ref, sin_ref, g_ref, ones_ref, swap_ref, o_ref, *,
                    heads_per_block, transpose_out):
    g, cos, sin = g_ref[...], cos_ref[...], sin_ref[...]
    for hh in range(heads_per_block):
        sl = slice(hh * HEAD_W, (hh + 1) * HEAD_W)
        x = x_ref[:, sl].astype(F32)
        ss = jnp.dot((x * x).astype(BF16), ones_ref[...], preferred_element_type=F32)
        y = (x * lax.rsqrt(ss * (1.0 / QK_DIM) + EPS)) * g
        partner = jnp.dot(jnp.concatenate(_split_terms(y, 2), axis=1), swap_ref[...],
                          preferred_element_type=F32)
        y = y * cos + partner * sin
        if transpose_out:
            o_ref[0, hh, 0] = y.T.astype(o_ref.dtype)
        else:
            o_ref[:, sl] = y.astype(o_ref.dtype)


def _qk_prep(proj, col0, tables, gain, tm, rows_per_seq, heads_per_block, transpose_out):
    m = proj.shape[0]
    nblk = rows_per_seq // tm
    tw = heads_per_block * HEAD_W
    lane_group = jnp.arange(HEAD_W) // QK_DIM
    group_ones = (lane_group[:, None] == lane_group[None, :]).astype(BF16)
    tab_spec = pl.BlockSpec((tm, HEAD_W), lambda i, j: (i % nblk, 0))
    lanes = jnp.arange(HEAD_W)
    in_rope = lanes % QK_DIM < ROPE_DIMS
    first_half = lanes % QK_DIM < ROPE_DIMS // 2
    partner_lane = jnp.where(first_half, lanes + ROPE_DIMS // 2, lanes - ROPE_DIMS // 2)
    swap = jnp.where(in_rope[None, :] & (lanes[:, None] == partner_lane[None, :]),
                     jnp.where(first_half, -1.0, 1.0)[None, :], 0.0).astype(BF16)
    swap = jnp.tile(swap, (2, 1))
    if transpose_out:
        out_spec = pl.BlockSpec((1, heads_per_block, 1, HEAD_W, tm),
                                lambda i, j: (i // nblk, j, i % nblk, 0, 0))
        out_shape = jax.ShapeDtypeStruct((m // rows_per_seq, HEADS, nblk, HEAD_W, tm), BF16)
    else:
        out_spec = pl.BlockSpec((tm, tw), lambda i, j: (i, j))
        out_shape = jax.ShapeDtypeStruct((m, HEADS * HEAD_W), BF16)
    return pl.pallas_call(
        functools.partial(_qk_prep_kernel, heads_per_block=heads_per_block,
                          transpose_out=transpose_out),
        grid=(m // tm, HEADS // heads_per_block),
        in_specs=[pl.BlockSpec((tm, tw), lambda i, j: (i, col0 * HEAD_W // tw + j)),
                  tab_spec, tab_spec,
                  pl.BlockSpec((1, HEAD_W), lambda i, j: (0, 0)),
                  pl.BlockSpec((HEAD_W, HEAD_W), lambda i, j: (0, 0)),
                  pl.BlockSpec((2 * HEAD_W, HEAD_W), lambda i, j: (0, 0))],
        out_specs=out_spec,
        out_shape=out_shape,
        compiler_params=_params(("parallel", "parallel"), tm * tw * 4 + 3 * tm * HEAD_W * 4,
                                tm * HEAD_W * 48),
        name="k_prep" if transpose_out else "q_prep",
    )(proj, *tables, gain.reshape(1, HEAD_W), group_ones, swap)


def _rope_tables(pos):
    half = ROPE_DIMS // 2
    inv_freq = ROPE_THETA ** (-(jnp.arange(half, dtype=F32) * 2.0) / ROPE_DIMS)
    ang = pos.astype(F32)[:, None] * inv_freq[None, :]
    cos, sin = jnp.cos(ang), jnp.sin(ang)
    n = pos.shape[0]
    pad = QK_DIM - ROPE_DIMS
    cos_t = jnp.concatenate([cos, cos, jnp.ones((n, pad), F32)], axis=1)
    sin_t = jnp.concatenate([sin, sin, jnp.zeros((n, pad), F32)], axis=1)
    return jnp.tile(cos_t, (1, 2)), jnp.tile(sin_t, (1, 2))


def _nt_dot(a, b):
    return lax.dot_general(a, b, (((1,), (1,)), ((), ())), preferred_element_type=F32)


def _lane_repeat(x, n):
    return jnp.concatenate([x] * n, axis=1)


def _attn_kernel(q_ref, kt_ref, v_ref, ktm_ref, vm_ref, lam_ref, subln_ref, o_ref,
                 vp_ref, vpm_ref, *, tq, nq):
    ones = jnp.ones((vp_ref.shape[0], HEAD_W), BF16)
    vp_ref[:, :HEAD_W] = v_ref[...]
    vp_ref[:, HEAD_W:] = ones
    vpm_ref[:, :HEAD_W] = vm_ref[...]
    vpm_ref[:, HEAD_W:] = ones[:META_ROWS]

    lp = lam_ref[...]
    lam = (jnp.exp(jnp.sum(lp[0:1] * lp[1:2], axis=-1, keepdims=True))
           - jnp.exp(jnp.sum(lp[2:3] * lp[3:4], axis=-1, keepdims=True)) + LAM_INIT)
    lane = lax.broadcasted_iota(jnp.int32, (tq, HEAD_W), 1)
    col_m = lax.broadcasted_iota(jnp.int32, (tq, META_ROWS), 1)
    row = lax.broadcasted_iota(jnp.int32, (tq, tq), 0)
    col = lax.broadcasted_iota(jnp.int32, (tq, tq), 1)

    def q_maps(qv):
        q = q_ref[qv * tq:(qv + 1) * tq, :]
        zero = jnp.zeros_like(q)
        return (jnp.where(lane < QK_DIM, q, zero),
                jnp.where(lane >= QK_DIM, q, zero))

    def scores(task):
        qv, j = task
        kt = ktm_ref[0, 0, 0] if j is None else kt_ref[0, 0, j]
        return [jnp.dot(qc, kt, preferred_element_type=F32) for qc in q_maps(qv)]

    def consume(state, s_maps, vp, mask):
        out = []
        for (m_old, acc), s in zip(state, s_maps):
            if mask is not None:
                s = jnp.where(mask, s, NEG_INF)
            m_new = jnp.maximum(m_old, jnp.max(s, axis=-1, keepdims=True))
            p = jnp.exp2((s - _lane_repeat(m_new, s.shape[1] // HEAD_W)).astype(BF16))
            alpha = jnp.exp2(m_old - m_new)
            acc = (_lane_repeat(alpha, 2) * acc
                   + jnp.dot(p, vp, preferred_element_type=F32))
            out.append((m_new, acc))
        return out

    def finalize(qv, state):
        a1, a2 = state[0][1], state[1][1]
        o = (a1[:, :HEAD_W] / a1[:, HEAD_W:HEAD_W + 1]
             - lam * (a2[:, :HEAD_W] / a2[:, HEAD_W:HEAD_W + 1]))
        r = lax.rsqrt(jnp.mean(o * o, axis=-1, keepdims=True) + EPS)
        o_ref[qv * tq:(qv + 1) * tq, :] = (
            ((o * r) * subln_ref[...]) * (1.0 - LAM_INIT)).astype(o_ref.dtype)

    tasks = [(qv, j) for qv in range(nq) for j in [None] + list(range(qv + 1))]
    s_next = scores(tasks[0])
    state = None
    for t, (qv, j) in enumerate(tasks):
        s_cur = s_next
        if t + 1 < len(tasks):
            s_next = scores(tasks[t + 1])
        if j is None:
            state = [(jnp.full((tq, HEAD_W), NEG_INF, F32), jnp.zeros((tq, 2 * HEAD_W), F32))] * 2
            state = consume(state, s_cur, vpm_ref[...], col_m >= META_ROWS - N_META)
        else:
            state = consume(state, s_cur, vp_ref[j * tq:(j + 1) * tq, :],
                            col <= row if j == qv else None)
            if j == qv:
                finalize(qv, state)


def _attention(q, kt, proj, kt_meta, proj_meta, lam_params, subln, batch, seq, tq):
    nq = seq // tq
    kernel = functools.partial(_attn_kernel, tq=tq, nq=nq)
    blk = 4 * seq * HEAD_W * 2 + 2 * META_ROWS * HEAD_W * 2
    scratch_bytes = (seq + META_ROWS) * 2 * HEAD_W * 2
    return pl.pallas_call(
        kernel,
        grid=(batch, HEADS),
        in_specs=[pl.BlockSpec((seq, HEAD_W), lambda b, h: (b, h)),
                  pl.BlockSpec((1, 1, nq, HEAD_W, tq), lambda b, h: (b, h, 0, 0, 0)),
                  pl.BlockSpec((seq, HEAD_W), lambda b, h: (b, COL_DA_V + h)),
                  pl.BlockSpec((1, 1, 1, HEAD_W, META_ROWS), lambda b, h: (0, h, 0, 0, 0)),
                  pl.BlockSpec((META_ROWS, HEAD_W), lambda b, h: (0, COL_DA_V + h)),
                  pl.BlockSpec((4, QK_DIM), lambda b, h: (0, 0)),
                  pl.BlockSpec((1, HEAD_W), lambda b, h: (0, 0))],
        out_specs=pl.BlockSpec((seq, HEAD_W), lambda b, h: (b, h)),
        out_shape=jax.ShapeDtypeStruct((batch * seq, HEADS * HEAD_W), BF16),
        scratch_shapes=[pltpu.VMEM((seq, 2 * HEAD_W), BF16),
                        pltpu.VMEM((META_ROWS, 2 * HEAD_W), BF16)],
        compiler_params=_params(("parallel", "parallel"), blk, scratch_bytes + 16 * tq * tq * 4),
        name="diff_attention",
    )(q, kt, proj, kt_meta, proj_meta, lam_params, subln)


def _lower_bound(lb_ref):
    x = lb_ref[...]
    e = jnp.exp(x - jnp.max(x, axis=0, keepdims=True))
    return e[0:1] / jnp.sum(e, axis=0, keepdims=True)


def _chunk_tri():
    r = jnp.arange(HG_CHUNK)
    return (r[None, :] <= r[:, None]).astype(BF16)


def _chunk_cumsum(tri, g):
    n = g.shape[0]
    parts = jnp.concatenate(_split_terms(g, 2), axis=2)
    s = lax.dot_general(jnp.broadcast_to(tri, (n,) + tri.shape), parts,
                        (((2,), (1,)), ((0,), (0,))), preferred_element_type=F32)
    return s[:, :, :HEAD_W] + s[:, :, HEAD_W:]


def _hg_block(q, fl, v, lb, tri, st, valid=None):
    c = HG_CHUNK
    n = q.shape[0] // c
    f = lb + (1.0 - lb) * _sigmoid(fl)
    g = jnp.log(f)
    kk = 1.0 - f
    if valid is not None:
        g = jnp.where(valid, g, 0.0)
        kk = jnp.where(valid, kk, 0.0)
        v = jnp.where(valid, v, jnp.zeros_like(v))
    b = _chunk_cumsum(tri, g.reshape(n, c, HEAD_W))
    q3, k3, v3 = q.reshape(n, c, HEAD_W), kk.reshape(n, c, HEAD_W), v.reshape(n, c, HEAD_W)
    b_mid = b[:, c // 2 - 1:c // 2, :]
    b_last = b[:, c - 1:c, :]
    q_in = (q3 * jnp.exp(b - b_mid)).astype(BF16)
    k_in = (k3 * jnp.exp(b_mid - b)).astype(BF16)
    k_dec = (k3 * jnp.exp(b_last - b)).astype(BF16)
    q_ex = (q3 * jnp.exp(b)).astype(BF16)
    decay = jnp.exp(b_last)
    row = lax.broadcasted_iota(jnp.int32, (n, c, c), 1)
    col = lax.broadcasted_iota(jnp.int32, (n, c, c), 2)
    a = lax.dot_general(q_in, k_in, (((2,), (2,)), ((0,), (0,))), preferred_element_type=F32)
    a = jnp.where(col <= row, a, 0.0).astype(BF16)
    o_intra = lax.dot_general(a, v3, (((2,), (1,)), ((0,), (0,))), preferred_element_type=F32)
    u_t = [lax.dot_general(v3[i], k_dec[i], (((0,), (0,)), ((), ())), preferred_element_type=F32)
           for i in range(n)]
    outs = []
    for i in range(n):
        outs.append(o_intra[i] + _nt_dot(q_ex[i], st.astype(BF16)))
        st = st * decay[i] + u_t[i]
    return outs, st


def _hg_meta_kernel(f_ref, i_ref, tri_ref, lb_ref, s_ref):
    row = lax.broadcasted_iota(jnp.int32, (HG_CHUNK, HEAD_W), 0)
    _, st = _hg_block(jnp.zeros((HG_CHUNK, HEAD_W), F32), f_ref[...].astype(F32), i_ref[...],
                      _lower_bound(lb_ref), tri_ref[...], jnp.zeros((HEAD_W, HEAD_W), F32),
                      valid=row >= HG_CHUNK - N_META)
    s_ref[0] = st


def _hg_meta_state(proj_meta, lower_bound):
    blk = META_ROWS // HG_CHUNK - 1
    return pl.pallas_call(
        _hg_meta_kernel,
        grid=(HEADS,),
        in_specs=[pl.BlockSpec((HG_CHUNK, HEAD_W), lambda h: (blk, COL_HG_F + h)),
                  pl.BlockSpec((HG_CHUNK, HEAD_W), lambda h: (blk, COL_HG_I + h)),
                  pl.BlockSpec((HG_CHUNK, HG_CHUNK), lambda h: (0, 0)),
                  pl.BlockSpec((lower_bound.shape[0], HEAD_W), lambda h: (0, h))],
        out_specs=pl.BlockSpec((1, HEAD_W, HEAD_W), lambda h: (h, 0, 0)),
        out_shape=jax.ShapeDtypeStruct((HEADS, HEAD_W, HEAD_W), F32),
        compiler_params=_params(("parallel",), 1 << 20),
        name="hgrn2_meta_state",
    )(proj_meta, proj_meta, _chunk_tri(), lower_bound)


def _hgrn_kernel(q_ref, f_ref, i_ref, g_ref, tri_ref, lb_ref, s0_ref, on_ref, o_ref, st_ref):
    @pl.when(pl.program_id(2) == 0)
    def _():
        st_ref[...] = s0_ref[0]

    outs, st = _hg_block(q_ref[...].astype(F32), f_ref[...].astype(F32), i_ref[...],
                         _lower_bound(lb_ref), tri_ref[...], st_ref[...])
    st_ref[...] = st
    for i, o in enumerate(outs):
        sl = pl.ds(i * HG_CHUNK, HG_CHUNK)
        r = lax.rsqrt(jnp.mean(o * o, axis=-1, keepdims=True) + EPS)
        y = ((o * r) * on_ref[...]) * _sigmoid(g_ref[sl, :].astype(F32))
        o_ref[sl, :] = y.astype(o_ref.dtype)


def _hgrn2(proj, s0, lower_bound, out_norm, batch, seq, tb):
    nt = seq // tb

    def col(off):
        return pl.BlockSpec((tb, HEAD_W), lambda b, h, t: (b * nt + t, off + h))

    return pl.pallas_call(
        _hgrn_kernel,
        grid=(batch, HEADS, nt),
        in_specs=[col(COL_HG_Q), col(COL_HG_F), col(COL_HG_I), col(COL_HG_G),
                  pl.BlockSpec((HG_CHUNK, HG_CHUNK), lambda b, h, t: (0, 0)),
                  pl.BlockSpec((lower_bound.shape[0], HEAD_W), lambda b, h, t: (0, h)),
                  pl.BlockSpec((1, HEAD_W, HEAD_W), lambda b, h, t: (h, 0, 0)),
                  pl.BlockSpec((1, HEAD_W), lambda b, h, t: (0, 0))],
        out_specs=pl.BlockSpec((tb, HEAD_W), lambda b, h, t: (b * nt + t, h)),
        out_shape=jax.ShapeDtypeStruct((batch * seq, HEADS * HEAD_W), BF16),
        scratch_shapes=[pltpu.VMEM((HEAD_W, HEAD_W), F32)],
        compiler_params=_params(("parallel", "parallel", "arbitrary"),
                                5 * tb * HEAD_W * 2, 24 * tb * HEAD_W * 4),
        name="hgrn2",
    )(proj, proj, proj, proj, _chunk_tri(), lower_bound, s0, out_norm)


def _merge_kernel(ya_ref, yb_ref, wa_ref, wb_ref, ga_ref, gb_ref, o_ref, wa_bf_ref, wb_bf_ref):
    @pl.when(pl.program_id(1) == 0)
    def _():
        wa_bf_ref[...] = wa_ref[...].astype(BF16)
        wb_bf_ref[...] = wb_ref[...].astype(BF16)

    a = jnp.dot(ya_ref[...], wa_bf_ref[...], preferred_element_type=F32)
    b = jnp.dot(yb_ref[...], wb_bf_ref[...], preferred_element_type=F32)
    o = (_sigmoid(ga_ref[...].astype(F32)) * a + _sigmoid(gb_ref[...].astype(F32)) * b)
    o_ref[...] = o.astype(o_ref.dtype)


def _gated_merge(y_a, y_b, w_a, w_b, proj, tm, tn):
    m = y_a.shape[0]
    kdim, n = w_a.shape
    tm = min(tm, m)
    blk = 2 * tm * kdim * 2 + 2 * kdim * tn * 4 + 3 * tm * tn * 2
    return pl.pallas_call(
        _merge_kernel,
        grid=(n // tn, m // tm),
        in_specs=[pl.BlockSpec((tm, kdim), lambda j, i: (i, 0)),
                  pl.BlockSpec((tm, kdim), lambda j, i: (i, 0)),
                  pl.BlockSpec((kdim, tn), lambda j, i: (0, j)),
                  pl.BlockSpec((kdim, tn), lambda j, i: (0, j)),
                  pl.BlockSpec((tm, tn), lambda j, i: (i, COL_GATE_A // tn + j)),
                  pl.BlockSpec((tm, tn), lambda j, i: (i, COL_GATE_B // tn + j))],
        out_specs=pl.BlockSpec((tm, tn), lambda j, i: (i, j)),
        out_shape=jax.ShapeDtypeStruct((m, n), BF16),
        scratch_shapes=[pltpu.VMEM((kdim, tn), BF16), pltpu.VMEM((kdim, tn), BF16)],
        compiler_params=_params(("arbitrary", "arbitrary"), blk,
                                2 * kdim * tn * 2 + 4 * tm * tn * 4),
        name="gated_merge",
    )(y_a, y_b, w_a, w_b, proj, proj)


def kernel(x, meta_tokens, norm_mix, w_in, da_q_norm, da_k_norm, da_lambda_q1, da_lambda_k1,
           da_lambda_q2, da_lambda_k2, da_subln, hg_lower_bound, hg_out_norm, w_up_a, w_up_b,
           w_out, norm_mlp, w_ff1, w_ff2):
    batch, seq, d = x.shape
    assert d == D_MODEL and w_in.shape[0] == 1 and w_in.shape[2] == IN_WIDTH
    assert meta_tokens.shape == (N_META, D_MODEL) and seq % 512 == 0
    m = batch * seq
    xf = x.reshape(m, d)
    meta_pad = jnp.concatenate([jnp.zeros((META_ROWS - N_META, d), F32), meta_tokens.astype(F32)])

    u = _rmsnorm(xf, norm_mix[0], 512)
    u_meta = _rmsnorm(meta_pad, norm_mix[0], META_ROWS)
    tn_in = 1024
    group = HEADS * HEAD_W // tn_in
    proj, proj_meta = _matmul_stream(
        u, w_in[0], tm=1024, tn=tn_in, out_dtype=BF16, name="in_proj",
        extra=(u_meta, [(COL_DA_K * HEAD_W // tn_in, 2 * group),
                        (COL_HG_F * HEAD_W // tn_in, 2 * group)]))

    q_gain = jnp.tile(da_q_norm[0], 2) * (QK_DIM ** -0.5 * math.log2(math.e))
    k_gain = jnp.tile(da_k_norm[0], 2)
    pos_meta = jnp.maximum(jnp.arange(META_ROWS, dtype=jnp.int32) - (META_ROWS - N_META), 0)
    tab_main = _rope_tables(jnp.arange(seq, dtype=jnp.int32) + N_META)
    q_hat = _qk_prep(proj, COL_DA_Q, tab_main, q_gain, ATT_BLOCK, seq, 8, False)
    kt = _qk_prep(proj, COL_DA_K, tab_main, k_gain, ATT_BLOCK, seq, 8, True)
    kt_meta = _qk_prep(proj_meta, COL_DA_K, _rope_tables(pos_meta), k_gain, META_ROWS, META_ROWS,
                       4, True)
    lam_params = jnp.stack([da_lambda_q1[0], da_lambda_k1[0], da_lambda_q2[0], da_lambda_k2[0]])
    y_a = _attention(q_hat, kt, proj, kt_meta, proj_meta, lam_params,
                     da_subln[0].reshape(1, HEAD_W), batch, seq, ATT_BLOCK)

    s0 = _hg_meta_state(proj_meta, hg_lower_bound)
    y_b = _hgrn2(proj, s0, hg_lower_bound, hg_out_norm[0].reshape(1, HEAD_W), batch, seq, seq)

    merged = _gated_merge(y_a, y_b, w_up_a[0], w_up_b[0], proj, 1024, 512)
    h1, h1_gained, h1_ssq = _matmul_stream(merged, w_out[0], tm=1024, tn=512, out_dtype=F32, res=xf,
                                           norm_gain=norm_mlp[0], name="out_proj")
    h1_ssq = _fold_lanes(h1_ssq, min(1024, m))

    hid, w_ff2_b = _matmul_stream(h1_gained, w_ff1[0], tm=1024, tn=1024, out_dtype=BF16, relu2=True,
                                  side=w_ff2[0], row_ssq=h1_ssq, name="ff1")
    out = _matmul_ktiled(hid, w_ff2_b, h1, tm=1024, tn=1024, tk=4096, name="ff2")
    return out.reshape(batch, seq, d)
```

```python
import functools
import math

import jax
import jax.numpy as jnp
from jax import lax
from jax.experimental import pallas as pl
from jax.experimental.pallas import tpu as pltpu

F32 = jnp.float32
BF16 = jnp.bfloat16

D_MODEL = 4096
N_META = 16
HEADS = 16
HEAD_W = 128
QK_DIM = 64
ROPE_DIMS = 16
ROPE_THETA = 500000.0
HG_CHUNK = 64
META_ROWS = 128
ATT_BLOCK = 512
MM_TILE = 1024
MM_TILE_NARROW = 512
NORM_ROWS = 512
EPS = 1e-6
NEG_INF = -1e30
LAM_INIT = 0.8 - 0.6 * math.exp(-0.3 * 0)

COL_DA_Q, COL_DA_K, COL_DA_V = 0, 16, 32
COL_HG_Q, COL_HG_F, COL_HG_I, COL_HG_G = 48, 64, 80, 96
COL_GATE_A, COL_GATE_B = 112 * HEAD_W, 144 * HEAD_W
IN_WIDTH = 176 * HEAD_W

V7X_VMEM_LIMIT_CAP = 58 * 2**20


def _params(semantics, block_bytes, extra_bytes=0):
    need = 2 * block_bytes + extra_bytes + (4 << 20)
    return pltpu.CompilerParams(dimension_semantics=semantics,
                                vmem_limit_bytes=min(max(need, 16 << 20), V7X_VMEM_LIMIT_CAP))


def _sigmoid(x):
    return 0.5 * jnp.tanh(0.5 * x) + 0.5


def _rmsnorm_kernel(x_ref, g_ref, o_ref):
    x = x_ref[...]
    r = lax.rsqrt(jnp.mean(x * x, axis=-1, keepdims=True) + EPS)
    o_ref[...] = ((x * r) * g_ref[...]).astype(o_ref.dtype)


def _rmsnorm(x, g, tm):
    m, d = x.shape
    return pl.pallas_call(
        _rmsnorm_kernel,
        grid=(m // tm,),
        in_specs=[pl.BlockSpec((tm, d), lambda i: (i, 0)),
                  pl.BlockSpec((1, d), lambda i: (0, 0))],
        out_specs=pl.BlockSpec((tm, d), lambda i: (i, 0)),
        out_shape=jax.ShapeDtypeStruct((m, d), BF16),
        compiler_params=_params(("parallel",), tm * d * 6, 2 * tm * d * 4),
        name="rmsnorm",
    )(x, g.reshape(1, d))


def _mm_ktiled_kernel(a_ref, w_ref, res_ref, o_ref):
    part = jnp.dot(a_ref[...], w_ref[...], preferred_element_type=F32)
    k = pl.program_id(2)

    @pl.when(k == 0)
    def _():
        o_ref[...] = part + res_ref[...]

    @pl.when(k > 0)
    def _():
        o_ref[...] += part


def _matmul_ktiled(a, w, res, *, tm, tn, tk, name):
    m, kdim = a.shape
    n = w.shape[1]
    tm = min(tm, m)
    block_bytes = tm * tk * 2 + tk * tn * 2 + 2 * tm * tn * 4
    return pl.pallas_call(
        _mm_ktiled_kernel,
        grid=(m // tm, n // tn, kdim // tk),
        in_specs=[pl.BlockSpec((tm, tk), lambda i, j, k: (i, k)),
                  pl.BlockSpec((tk, tn), lambda i, j, k: (k, j)),
                  pl.BlockSpec((tm, tn), lambda i, j, k: (i, j))],
        out_specs=pl.BlockSpec((tm, tn), lambda i, j, k: (i, j)),
        out_shape=jax.ShapeDtypeStruct((m, n), F32),
        compiler_params=_params(("parallel", "parallel", "arbitrary"), block_bytes,
                                2 * tm * tn * 4),
        name=name,
    )(a, w, res)


def _mm_stream_kernel(*refs, relu2, has_res, has_side, extra_cols, row_ssq, emit_norm, n_col,
                      n_row):
    refs = list(refs)
    a_ref, w_hbm = refs.pop(0), refs.pop(0)
    res_ref = refs.pop(0) if has_res else None
    side_hbm = refs.pop(0) if has_side else None
    extra_a_ref = refs.pop(0) if extra_cols else None
    ssq_in_ref = refs.pop(0) if row_ssq else None
    gain_ref = refs.pop(0) if emit_norm else None
    o_ref = refs.pop(0)
    side_out_hbm = refs.pop(0) if has_side else None
    extra_o_ref = refs.pop(0) if extra_cols else None
    scaled_o_ref, ssq_o_ref = (refs.pop(0), refs.pop(0)) if emit_norm else (None, None)
    wb_ref, stage_ref, sem = refs.pop(0), refs.pop(0), refs.pop(0)
    j, i = pl.program_id(0), pl.program_id(1)
    ck, tn = stage_ref.shape[1], stage_ref.shape[2]

    def chunk_copy(col, c, slot):
        return pltpu.make_async_copy(
            w_hbm.at[pl.ds(pl.multiple_of(c * ck, ck), ck), pl.ds(pl.multiple_of(col * tn, tn), tn)],
            stage_ref.at[slot], sem.at[slot])

    def lookahead_col(jj):
        return jnp.minimum(jj + 1, n_col - 1)

    @pl.when((j == 0) & (i == 0))
    def _():
        for c in range(n_row):
            cp = chunk_copy(0, c, c % 2)
            cp.start()
            cp.wait()
            wb_ref[0, c * ck:(c + 1) * ck, :] = stage_ref[c % 2].astype(BF16)
        chunk_copy(lookahead_col(0), 0, 0).start()

    acc = jnp.dot(a_ref[...], wb_ref[j % 2], preferred_element_type=F32)
    if relu2:
        acc = jnp.square(jnp.maximum(acc, 0.0))
    if row_ssq:
        mean_sq = jnp.sum(ssq_in_ref[...], axis=-1, keepdims=True) * (1.0 / a_ref.shape[1])
        acc = acc * (1.0 / (mean_sq + EPS))
    if has_res:
        acc = acc + res_ref[...]
    o_ref[...] = acc.astype(o_ref.dtype)
    if emit_norm:
        scaled_o_ref[...] = (acc * gain_ref[...]).astype(scaled_o_ref.dtype)
        sq = acc * acc
        part = sq[:, :HEAD_W]
        for c in range(1, tn // HEAD_W):
            part = part + sq[:, c * HEAD_W:(c + 1) * HEAD_W]
        ssq_o_ref[...] = part

    chunk_copy(lookahead_col(j), i, i % 2).wait()
    wb_ref[(j + 1) % 2, pl.ds(pl.multiple_of(i * ck, ck), ck), :] = stage_ref[i % 2].astype(BF16)

    @pl.when((j < n_col - 1) | (i < n_row - 1))
    def _():
        wrap = i == n_row - 1
        chunk_copy(lookahead_col(jnp.where(wrap, j + 1, j)), jnp.where(wrap, 0, i + 1),
                   (i + 1) % 2).start()

    if extra_cols:
        wanted = functools.reduce(jnp.logical_or,
                                  [(j >= lo) & (j < lo + cnt) for lo, cnt in extra_cols])

        @pl.when((i == 0) & wanted)
        def _():
            extra_o_ref[...] = jnp.dot(extra_a_ref[...], wb_ref[j % 2],
                                       preferred_element_type=F32).astype(extra_o_ref.dtype)

        @pl.when((i == 0) & jnp.logical_not(wanted))
        def _():
            extra_o_ref[...] = jnp.zeros_like(extra_o_ref)

    if has_side:
        in_buf, out_buf, in_sem, out_sem = refs
        rs = in_buf.shape[1]
        t, n_steps = j * n_row + i, n_col * n_row

        def side_in(tt):
            return pltpu.make_async_copy(side_hbm.at[pl.ds(pl.multiple_of(tt * rs, rs), rs), :],
                                         in_buf.at[tt % 2], in_sem.at[tt % 2])

        def side_out(tt):
            return pltpu.make_async_copy(out_buf.at[tt % 2],
                                         side_out_hbm.at[pl.ds(pl.multiple_of(tt * rs, rs), rs), :],
                                         out_sem.at[tt % 2])

        @pl.when(t == 0)
        def _():
            side_in(t).start()

        @pl.when(t + 1 < n_steps)
        def _():
            side_in(t + 1).start()

        side_in(t).wait()

        @pl.when(t >= 2)
        def _():
            side_out(t - 2).wait()

        out_buf[t % 2] = in_buf[t % 2].astype(BF16)
        side_out(t).start()

        @pl.when(t == n_steps - 1)
        def _():
            side_out(t - 1).wait()
            side_out(t).wait()


def _matmul_stream(a, w, *, tm, tn, out_dtype, relu2=False, res=None, side=None, extra=None,
                   row_ssq=None, norm_gain=None, name):
    m, kdim = a.shape
    n = w.shape[1]
    n_row, n_col = m // tm, n // tn
    assert n_row % 2 == 0 and n_col >= 2 and kdim % n_row == 0
    ck = kdim // n_row
    in_specs = [pl.BlockSpec((tm, kdim), lambda j, i: (i, 0)),
                pl.BlockSpec(memory_space=pl.ANY)]
    args = [a, w]
    out_bytes = jnp.dtype(out_dtype).itemsize
    block_bytes = tm * kdim * 2 + tm * tn * out_bytes
    if res is not None:
        in_specs.append(pl.BlockSpec((tm, tn), lambda j, i: (i, j)))
        args.append(res)
        block_bytes += tm * tn * 4
    out_specs = [pl.BlockSpec((tm, tn), lambda j, i: (i, j))]
    out_shape = [jax.ShapeDtypeStruct((m, n), out_dtype)]
    scratch = [pltpu.VMEM((2, kdim, tn), BF16), pltpu.VMEM((2, ck, tn), F32),
               pltpu.SemaphoreType.DMA((2,))]
    scratch_bytes = 2 * kdim * tn * 2 + 2 * ck * tn * 4
    if side is not None:
        rows, cols = side.shape
        assert rows % (n_row * n_col) == 0
        rs = rows // (n_row * n_col)
        in_specs.append(pl.BlockSpec(memory_space=pl.ANY))
        args.append(side)
        out_specs.append(pl.BlockSpec(memory_space=pl.ANY))
        out_shape.append(jax.ShapeDtypeStruct(side.shape, BF16))
        scratch += [pltpu.VMEM((2, rs, cols), F32), pltpu.VMEM((2, rs, cols), BF16),
                    pltpu.SemaphoreType.DMA((2,)), pltpu.SemaphoreType.DMA((2,))]
        scratch_bytes += 2 * rs * cols * 6
    extra_cols = None
    if extra is not None:
        extra_rows, extra_cols = extra
        r = extra_rows.shape[0]
        in_specs.append(pl.BlockSpec((r, kdim), lambda j, i: (0, 0)))
        args.append(extra_rows)
        out_specs.append(pl.BlockSpec((r, tn), lambda j, i: (0, j)))
        out_shape.append(jax.ShapeDtypeStruct((r, n), BF16))
        block_bytes += r * kdim * 2 + r * tn * 2
    if row_ssq is not None:
        assert relu2
        in_specs.append(pl.BlockSpec((tm, HEAD_W), lambda j, i: (i, 0)))
        args.append(row_ssq)
        block_bytes += tm * HEAD_W * 4
        scratch_bytes += tm * tn * 4
    if norm_gain is not None:
        in_specs.append(pl.BlockSpec((1, tn), lambda j, i: (0, j)))
        args.append(norm_gain.reshape(1, n))
        out_specs += [pl.BlockSpec((tm, tn), lambda j, i: (i, j)),
                      pl.BlockSpec((tm, HEAD_W), lambda j, i: (i, j))]
        out_shape += [jax.ShapeDtypeStruct((m, n), BF16),
                      jax.ShapeDtypeStruct((m, n_col * HEAD_W), F32)]
        block_bytes += tm * tn * 2 + tm * HEAD_W * 4
    results = pl.pallas_call(
        functools.partial(_mm_stream_kernel, relu2=relu2, has_res=res is not None,
                          has_side=side is not None, extra_cols=extra_cols,
                          row_ssq=row_ssq is not None, emit_norm=norm_gain is not None,
                          n_col=n_col, n_row=n_row),
        grid=(n_col, n_row),
        in_specs=in_specs,
        out_specs=out_specs,
        out_shape=out_shape,
        scratch_shapes=scratch,
        compiler_params=_params(("arbitrary", "arbitrary"), block_bytes,
                                scratch_bytes + tm * tn * 4),
        name=name,
    )(*args)
    return results[0] if len(results) == 1 else tuple(results)


def _fold_lanes_kernel(x_ref, o_ref):
    x = x_ref[...]
    acc = x[:, :HEAD_W]
    for c in range(1, x.shape[1] // HEAD_W):
        acc = acc + x[:, c * HEAD_W:(c + 1) * HEAD_W]
    o_ref[...] = acc


def _fold_lanes(x, tm):
    m, w = x.shape
    return pl.pallas_call(
        _fold_lanes_kernel,
        grid=(m // tm,),
        in_specs=[pl.BlockSpec((tm, w), lambda i: (i, 0))],
        out_specs=pl.BlockSpec((tm, HEAD_W), lambda i: (i, 0)),
        out_shape=jax.ShapeDtypeStruct((m, HEAD_W), F32),
        compiler_params=_params(("parallel",), tm * (w + HEAD_W) * 4),
        name="fold_lanes",
    )(x)


def _split_terms(x, n):
    terms = []
    for _ in range(n - 1):
        t = x.astype(BF16)
        terms.append(t)
        x = x - t.astype(F32)
    terms.append(x.astype(BF16))
    return terms


def _qk_prep_kernel(x_ref, cos_ref, sin_ref, g_ref, ones_ref, swap_ref, o_ref, *,
                    heads_per_block, transpose_out):
    g, cos, sin = g_ref[...], cos_ref[...], sin_ref[...]
    for hh in range(heads_per_block):
        sl = slice(hh * HEAD_W, (hh + 1) * HEAD_W)
        x = x_ref[:, sl].astype(F32)
        ss = jnp.dot((x * x).astype(BF16), ones_ref[...], preferred_element_type=F32)
        y = (x * lax.rsqrt(ss * (1.0 / QK_DIM) + EPS)) * g
        partner = jnp.dot(jnp.concatenate(_split_terms(y, 2), axis=1), swap_ref[...],
                          preferred_element_type=F32)
        y = y * cos + partner * sin
        if transpose_out:
            o_ref[0, hh, 0] = y.T.astype(o_ref.dtype)
        else:
            o_ref[:, sl] = y.astype(o_ref.dtype)


def _qk_prep(proj, col0, tables, gain, tm, rows_per_seq, heads_per_block, transpose_out):
    m = proj.shape[0]
    nblk = rows_per_seq // tm
    tw = heads_per_block * HEAD_W
    lane_group = jnp.arange(HEAD_W) // QK_DIM
    group_ones = (lane_group[:, None] == lane_group[None, :]).astype(BF16)
    tab_spec = pl.BlockSpec((tm, HEAD_W), lambda i, j: (i % nblk, 0))
    lanes = jnp.arange(HEAD_W)
    in_rope = lanes % QK_DIM < ROPE_DIMS
    first_half = lanes % QK_DIM < ROPE_DIMS // 2
    partner_lane = jnp.where(first_half, lanes + ROPE_DIMS // 2, lanes - ROPE_DIMS // 2)
    swap = jnp.where(in_rope[None, :] & (lanes[:, None] == partner_lane[None, :]),
                     jnp.where(first_half, -1.0, 1.0)[None, :], 0.0).astype(BF16)
    swap = jnp.tile(swap, (2, 1))
    if transpose_out:
        out_spec = pl.BlockSpec((1, heads_per_block, 1, HEAD_W, tm),
                                lambda i, j: (i // nblk, j, i % nblk, 0, 0))
        out_shape = jax.ShapeDtypeStruct((m // rows_per_seq, HEADS, nblk, HEAD_W, tm), BF16)
    else:
        out_spec = pl.BlockSpec((tm, tw), lambda i, j: (i, j))
        out_shape = jax.ShapeDtypeStruct((m, HEADS * HEAD_W), BF16)
    return pl.pallas_call(
        functools.partial(_qk_prep_kernel, heads_per_block=heads_per_block,
                          transpose_out=transpose_out),
        grid=(m // tm, HEADS // heads_per_block),
        in_specs=[pl.BlockSpec((tm, tw), lambda i, j: (i, col0 * HEAD_W // tw + j)),
                  tab_spec, tab_spec,
                  pl.BlockSpec((1, HEAD_W), lambda i, j: (0, 0)),
                  pl.BlockSpec((HEAD_W, HEAD_W), lambda i, j: (0, 0)),
                  pl.BlockSpec((2 * HEAD_W, HEAD_W), lambda i, j: (0, 0))],
        out_specs=out_spec,
        out_shape=out_shape,
        compiler_params=_params(("parallel", "parallel"), tm * tw * 4 + 3 * tm * HEAD_W * 4,
                                tm * HEAD_W * 48),
        name="k_prep" if transpose_out else "q_prep",
    )(proj, *tables, gain.reshape(1, HEAD_W), group_ones, swap)


def _rope_tables(pos):
    half = ROPE_DIMS // 2
    inv_freq = ROPE_THETA ** (-(jnp.arange(half, dtype=F32) * 2.0) / ROPE_DIMS)
    ang = pos.astype(F32)[:, None] * inv_freq[None, :]
    cos, sin = jnp.cos(ang), jnp.sin(ang)
    n = pos.shape[0]
    pad = QK_DIM - ROPE_DIMS
    cos_t = jnp.concatenate([cos, cos, jnp.ones((n, pad), F32)], axis=1)
    sin_t = jnp.concatenate([sin, sin, jnp.zeros((n, pad), F32)], axis=1)
    return jnp.tile(cos_t, (1, 2)), jnp.tile(sin_t, (1, 2))


def _nt_dot(a, b):
    return lax.dot_general(a, b, (((1,), (1,)), ((), ())), preferred_element_type=F32)


def _lane_repeat(x, n):
    return jnp.concatenate([x] * n, axis=1)


def _attn_kernel(q_ref, kt_ref, v_ref, ktm_ref, vm_ref, lam_ref, subln_ref, o_ref,
                 vp_ref, vpm_ref, *, tq, nq):
    ones = jnp.ones((vp_ref.shape[0], HEAD_W), BF16)
    vp_ref[:, :HEAD_W] = v_ref[...]
    vp_ref[:, HEAD_W:] = ones
    vpm_ref[:, :HEAD_W] = vm_ref[...]
    vpm_ref[:, HEAD_W:] = ones[:META_ROWS]

    lp = lam_ref[...]
    lam = (jnp.exp(jnp.sum(lp[0:1] * lp[1:2], axis=-1, keepdims=True))
           - jnp.exp(jnp.sum(lp[2:3] * lp[3:4], axis=-1, keepdims=True)) + LAM_INIT)
    lane = lax.broadcasted_iota(jnp.int32, (tq, HEAD_W), 1)
    col_m = lax.broadcasted_iota(jnp.int32, (tq, META_ROWS), 1)
    row = lax.broadcasted_iota(jnp.int32, (tq, tq), 0)
    col = lax.broadcasted_iota(jnp.int32, (tq, tq), 1)

    def q_maps(qv):
        q = q_ref[qv * tq:(qv + 1) * tq, :]
        zero = jnp.zeros_like(q)
        return (jnp.where(lane < QK_DIM, q, zero),
                jnp.where(lane >= QK_DIM, q, zero))

    def scores(task):
        qv, j = task
        kt = ktm_ref[0, 0, 0] if j is None else kt_ref[0, 0, j]
        return [jnp.dot(qc, kt, preferred_element_type=F32) for qc in q_maps(qv)]

    def consume(state, s_maps, vp, mask):
        out = []
        for (m_old, acc), s in zip(state, s_maps):
            if mask is not None:
                s = jnp.where(mask, s, NEG_INF)
            m_new = jnp.maximum(m_old, jnp.max(s, axis=-1, keepdims=True))
            p = jnp.exp2((s - _lane_repeat(m_new, s.shape[1] // HEAD_W)).astype(BF16))
            alpha = jnp.exp2(m_old - m_new)
            acc = (_lane_repeat(alpha, 2) * acc
                   + jnp.dot(p, vp, preferred_element_type=F32))
            out.append((m_new, acc))
        return out

    def finalize(qv, state):
        a1, a2 = state[0][1], state[1][1]
        o = (a1[:, :HEAD_W] / a1[:, HEAD_W:HEAD_W + 1]
             - lam * (a2[:, :HEAD_W] / a2[:, HEAD_W:HEAD_W + 1]))
        r = lax.rsqrt(jnp.mean(o * o, axis=-1, keepdims=True) + EPS)
        o_ref[qv * tq:(qv + 1) * tq, :] = (
            ((o * r) * subln_ref[...]) * (1.0 - LAM_INIT)).astype(o_ref.dtype)

    tasks = [(qv, j) for qv in range(nq) for j in [None] + list(range(qv + 1))]
    s_next = scores(tasks[0])
    state = None
    for t, (qv, j) in enumerate(tasks):
        s_cur = s_next
        if t + 1 < len(tasks):
            s_next = scores(tasks[t + 1])
        if j is None:
            state = [(jnp.full((tq, HEAD_W), NEG_INF, F32), jnp.zeros((tq, 2 * HEAD_W), F32))] * 2
            state = consume(state, s_cur, vpm_ref[...], col_m >= META_ROWS - N_META)
        else:
            state = consume(state, s_cur, vp_ref[j * tq:(j + 1) * tq, :],
                            col <= row if j == qv else None)
            if j == qv:
                finalize(qv, state)


def _attention(q, kt, proj, kt_meta, proj_meta, lam_params, subln, batch, seq, tq):
    nq = seq // tq
    kernel = functools.partial(_attn_kernel, tq=tq, nq=nq)
    blk = 4 * seq * HEAD_W * 2 + 2 * META_ROWS * HEAD_W * 2
    scratch_bytes = (seq + META_ROWS) * 2 * HEAD_W * 2
    return pl.pallas_call(
        kernel,
        grid=(batch, HEADS),
        in_specs=[pl.BlockSpec((seq, HEAD_W), lambda b, h: (b, h)),
                  pl.BlockSpec((1, 1, nq, HEAD_W, tq), lambda b, h: (b, h, 0, 0, 0)),
                  pl.BlockSpec((seq, HEAD_W), lambda b, h: (b, COL_DA_V + h)),
                  pl.BlockSpec((1, 1, 1, HEAD_W, META_ROWS), lambda b, h: (0, h, 0, 0, 0)),
                  pl.BlockSpec((META_ROWS, HEAD_W), lambda b, h: (0, COL_DA_V + h)),
                  pl.BlockSpec((4, QK_DIM), lambda b, h: (0, 0)),
                  pl.BlockSpec((1, HEAD_W), lambda b, h: (0, 0))],
        out_specs=pl.BlockSpec((seq, HEAD_W), lambda b, h: (b, h)),
        out_shape=jax.ShapeDtypeStruct((batch * seq, HEADS * HEAD_W), BF16),
        scratch_shapes=[pltpu.VMEM((seq, 2 * HEAD_W), BF16),
                        pltpu.VMEM((META_ROWS, 2 * HEAD_W), BF16)],
        compiler_params=_params(("parallel", "parallel"), blk, scratch_bytes + 16 * tq * tq * 4),
        name="diff_attention",
    )(q, kt, proj, kt_meta, proj_meta, lam_params, subln)


def _lower_bound(lb_ref):
    x = lb_ref[...]
    e = jnp.exp(x - jnp.max(x, axis=0, keepdims=True))
    return e[0:1] / jnp.sum(e, axis=0, keepdims=True)


def _chunk_tri():
    r = jnp.arange(HG_CHUNK)
    return (r[None, :] <= r[:, None]).astype(BF16)


def _chunk_cumsum(tri, g):
    n = g.shape[0]
    parts = jnp.concatenate(_split_terms(g, 2), axis=2)
    s = lax.dot_general(jnp.broadcast_to(tri, (n,) + tri.shape), parts,
                        (((2,), (1,)), ((0,), (0,))), preferred_element_type=F32)
    return s[:, :, :HEAD_W] + s[:, :, HEAD_W:]


def _hg_block(q, fl, v, lb, tri, st, valid=None):
    c = HG_CHUNK
    n = q.shape[0] // c
    f = lb + (1.0 - lb) * _sigmoid(fl)
    g = jnp.log(f)
    kk = 1.0 - f
    if valid is not None:
        g = jnp.where(valid, g, 0.0)
        kk = jnp.where(valid, kk, 0.0)
        v = jnp.where(valid, v, jnp.zeros_like(v))
    b = _chunk_cumsum(tri, g.reshape(n, c, HEAD_W))
    q3, k3, v3 = q.reshape(n, c, HEAD_W), kk.reshape(n, c, HEAD_W), v.reshape(n, c, HEAD_W)
    b_mid = b[:, c // 2 - 1:c // 2, :]
    b_last = b[:, c - 1:c, :]
    q_in = (q3 * jnp.exp(b - b_mid)).astype(BF16)
    k_in = (k3 * jnp.exp(b_mid - b)).astype(BF16)
    k_dec = (k3 * jnp.exp(b_last - b)).astype(BF16)
    q_ex = (q3 * jnp.exp(b)).astype(BF16)
    decay = jnp.exp(b_last)
    row = lax.broadcasted_iota(jnp.int32, (n, c, c), 1)
    col = lax.broadcasted_iota(jnp.int32, (n, c, c), 2)
    a = lax.dot_general(q_in, k_in, (((2,), (2,)), ((0,), (0,))), preferred_element_type=F32)
    a = jnp.where(col <= row, a, 0.0).astype(BF16)
    o_intra = lax.dot_general(a, v3, (((2,), (1,)), ((0,), (0,))), preferred_element_type=F32)
    u_t = [lax.dot_general(v3[i], k_dec[i], (((0,), (0,)), ((), ())), preferred_element_type=F32)
           for i in range(n)]
    outs = []
    for i in range(n):
        outs.append(o_intra[i] + _nt_dot(q_ex[i], st.astype(BF16)))
        st = st * decay[i] + u_t[i]
    return outs, st


def _hg_meta_kernel(f_ref, i_ref, tri_ref, lb_ref, s_ref):
    row = lax.broadcasted_iota(jnp.int32, (HG_CHUNK, HEAD_W), 0)
    _, st = _hg_block(jnp.zeros((HG_CHUNK, HEAD_W), F32), f_ref[...].astype(F32), i_ref[...],
                      _lower_bound(lb_ref), tri_ref[...], jnp.zeros((HEAD_W, HEAD_W), F32),
                      valid=row >= HG_CHUNK - N_META)
    s_ref[0] = st


def _hg_meta_state(proj_meta, lower_bound):
    blk = META_ROWS // HG_CHUNK - 1
    return pl.pallas_call(
        _hg_meta_kernel,
        grid=(HEADS,),
        in_specs=[pl.BlockSpec((HG_CHUNK, HEAD_W), lambda h: (blk, COL_HG_F + h)),
                  pl.BlockSpec((HG_CHUNK, HEAD_W), lambda h: (blk, COL_HG_I + h)),
                  pl.BlockSpec((HG_CHUNK, HG_CHUNK), lambda h: (0, 0)),
                  pl.BlockSpec((lower_bound.shape[0], HEAD_W), lambda h: (0, h))],
        out_specs=pl.BlockSpec((1, HEAD_W, HEAD_W), lambda h: (h, 0, 0)),
        out_shape=jax.ShapeDtypeStruct((HEADS, HEAD_W, HEAD_W), F32),
        compiler_params=_params(("parallel",), 1 << 20),
        name="hgrn2_meta_state",
    )(proj_meta, proj_meta, _chunk_tri(), lower_bound)


def _hgrn_kernel(q_ref, f_ref, i_ref, g_ref, tri_ref, lb_ref, s0_ref, on_ref, o_ref, st_ref):
    @pl.when(pl.program_id(2) == 0)
    def _():
        st_ref[...] = s0_ref[0]

    outs, st = _hg_block(q_ref[...].astype(F32), f_ref[...].astype(F32), i_ref[...],
                         _lower_bound(lb_ref), tri_ref[...], st_ref[...])
    st_ref[...] = st
    for i, o in enumerate(outs):
        sl = pl.ds(i * HG_CHUNK, HG_CHUNK)
        r = lax.rsqrt(jnp.mean(o * o, axis=-1, keepdims=True) + EPS)
        y = ((o * r) * on_ref[...]) * _sigmoid(g_ref[sl, :].astype(F32))
        o_ref[sl, :] = y.astype(o_ref.dtype)


def _hgrn2(proj, s0, lower_bound, out_norm, batch, seq, tb):
    nt = seq // tb

    def col(off):
        return pl.BlockSpec((tb, HEAD_W), lambda b, h, t: (b * nt + t, off + h))

    return pl.pallas_call(
        _hgrn_kernel,
        grid=(batch, HEADS, nt),
        in_specs=[col(COL_HG_Q), col(COL_HG_F), col(COL_HG_I), col(COL_HG_G),
                  pl.BlockSpec((HG_CHUNK, HG_CHUNK), lambda b, h, t: (0, 0)),
                  pl.BlockSpec((lower_bound.shape[0], HEAD_W), lambda b, h, t: (0, h)),
                  pl.BlockSpec((1, HEAD_W, HEAD_W), lambda b, h, t: (h, 0, 0)),
                  pl.BlockSpec((1, HEAD_W), lambda b, h, t: (0, 0))],
        out_specs=pl.BlockSpec((tb, HEAD_W), lambda b, h, t: (b * nt + t, h)),
        out_shape=jax.ShapeDtypeStruct((batch * seq, HEADS * HEAD_W), BF16),
        scratch_shapes=[pltpu.VMEM((HEAD_W, HEAD_W), F32)],
        compiler_params=_params(("parallel", "parallel", "arbitrary"),
                                5 * tb * HEAD_W * 2, 24 * tb * HEAD_W * 4),
        name="hgrn2",
    )(proj, proj, proj, proj, _chunk_tri(), lower_bound, s0, out_norm)


def _merge_kernel(ya_ref, yb_ref, wa_ref, wb_ref, ga_ref, gb_ref, o_ref, wa_bf_ref, wb_bf_ref):
    @pl.when(pl.program_id(1) == 0)
    def _():
        wa_bf_ref[...] = wa_ref[...].astype(BF16)
        wb_bf_ref[...] = wb_ref[...].astype(BF16)

    a = jnp.dot(ya_ref[...], wa_bf_ref[...], preferred_element_type=F32)
    b = jnp.dot(yb_ref[...], wb_bf_ref[...], preferred_element_type=F32)
    o = (_sigmoid(ga_ref[...].astype(F32)) * a + _sigmoid(gb_ref[...].astype(F32)) * b)
    o_ref[...] = o.astype(o_ref.dtype)


def _gated_merge(y_a, y_b, w_a, w_b, proj, tm, tn):
    m = y_a.shape[0]
    kdim, n = w_a.shape
    tm = min(tm, m)
    blk = 2 * tm * kdim * 2 + 2 * kdim * tn * 4 + 3 * tm * tn * 2
    return pl.pallas_call(
        _merge_kernel,
        grid=(n // tn, m // tm),
        in_specs=[pl.BlockSpec((tm, kdim), lambda j, i: (i, 0)),
                  pl.BlockSpec((tm, kdim), lambda j, i: (i, 0)),
                  pl.BlockSpec((kdim, tn), lambda j, i: (0, j)),
                  pl.BlockSpec((kdim, tn), lambda j, i: (0, j)),
                  pl.BlockSpec((tm, tn), lambda j, i: (i, COL_GATE_A // tn + j)),
                  pl.BlockSpec((tm, tn), lambda j, i: (i, COL_GATE_B // tn + j))],
        out_specs=pl.BlockSpec((tm, tn), lambda j, i: (i, j)),
        out_shape=jax.ShapeDtypeStruct((m, n), BF16),
        scratch_shapes=[pltpu.VMEM((kdim, tn), BF16), pltpu.VMEM((kdim, tn), BF16)],
        compiler_params=_params(("arbitrary", "arbitrary"), blk,
                                2 * kdim * tn * 2 + 4 * tm * tn * 4),
        name="gated_merge",
    )(y_a, y_b, w_a, w_b, proj, proj)


def kernel(x, meta_tokens, norm_mix, w_in, da_q_norm, da_k_norm, da_lambda_q1, da_lambda_k1,
           da_lambda_q2, da_lambda_k2, da_subln, hg_lower_bound, hg_out_norm, w_up_a, w_up_b,
           w_out, norm_mlp, w_ff1, w_ff2):
    batch, seq, d = x.shape
    assert d == D_MODEL and w_in.shape[0] == 1 and w_in.shape[2] == IN_WIDTH
    assert meta_tokens.shape == (N_META, D_MODEL) and seq % ATT_BLOCK == 0
    m = batch * seq
    xf = x.reshape(m, d)
    meta_pad = jnp.concatenate([jnp.zeros((META_ROWS - N_META, d), F32), meta_tokens.astype(F32)])

    u = _rmsnorm(xf, norm_mix[0], NORM_ROWS)
    u_meta = _rmsnorm(meta_pad, norm_mix[0], META_ROWS)
    tn_in = MM_TILE
    group = HEADS * HEAD_W // tn_in
    proj, proj_meta = _matmul_stream(
        u, w_in[0], tm=MM_TILE, tn=tn_in, out_dtype=BF16, name="in_proj",
        extra=(u_meta, [(COL_DA_K * HEAD_W // tn_in, 2 * group),
                        (COL_HG_F * HEAD_W // tn_in, 2 * group)]))

    q_gain = jnp.tile(da_q_norm[0], 2) * (QK_DIM ** -0.5 * math.log2(math.e))
    k_gain = jnp.tile(da_k_norm[0], 2)
    pos_meta = jnp.maximum(jnp.arange(META_ROWS, dtype=jnp.int32) - (META_ROWS - N_META), 0)
    tab_main = _rope_tables(jnp.arange(seq, dtype=jnp.int32) + N_META)
    q_hat = _qk_prep(proj, COL_DA_Q, tab_main, q_gain, ATT_BLOCK, seq, HEADS, False)
    kt = _qk_prep(proj, COL_DA_K, tab_main, k_gain, ATT_BLOCK, seq, HEADS, True)
    kt_meta = _qk_prep(proj_meta, COL_DA_K, _rope_tables(pos_meta), k_gain, META_ROWS, META_ROWS,
                       4, True)
    lam_params = jnp.stack([da_lambda_q1[0], da_lambda_k1[0], da_lambda_q2[0], da_lambda_k2[0]])
    y_a = _attention(q_hat, kt, proj, kt_meta, proj_meta, lam_params,
                     da_subln[0].reshape(1, HEAD_W), batch, seq, ATT_BLOCK)

    s0 = _hg_meta_state(proj_meta, hg_lower_bound)
    y_b = _hgrn2(proj, s0, hg_lower_bound, hg_out_norm[0].reshape(1, HEAD_W), batch, seq, seq)

    merged = _gated_merge(y_a, y_b, w_up_a[0], w_up_b[0], proj, MM_TILE, MM_TILE_NARROW)
    h1, h1_gained, h1_ssq = _matmul_stream(merged, w_out[0], tm=MM_TILE, tn=MM_TILE_NARROW,
                                           out_dtype=F32, res=xf, norm_gain=norm_mlp[0],
                                           name="out_proj")
    h1_ssq = _fold_lanes(h1_ssq, min(MM_TILE, m))

    hid, w_ff2_b = _matmul_stream(h1_gained, w_ff1[0], tm=MM_TILE, tn=MM_TILE, out_dtype=BF16,
                                  relu2=True, side=w_ff2[0], row_ssq=h1_ssq, name="ff1")
    out = _matmul_ktiled(hid, w_ff2_b, h1, tm=MM_TILE, tn=MM_TILE, tk=4 * MM_TILE, name="ff2")
    return out.reshape(batch, seq, d)
```
